```python
import jax, jax.numpy as jnp
from jax import lax
import numpy as np

D_MODEL = 2048
BATCH = 4
SEQ = 2048
DEPTH = 4
DEC_BATCH = 128
DEC_SEQ = 4
PAST_LEN = 16384
PAGE_SIZE = 128

N_EVEN = (DEPTH + 1) // 2
N_ODD = DEPTH // 2
A_CHUNK = 128
A_HEADS = 4
A_WIDTH = D_MODEL // 2
A_HEAD_DIM = A_WIDTH // A_HEADS
B_HEADS = 4
B_VAL_WIDTH = D_MODEL // 2
B_KEY_WIDTH = B_VAL_WIDTH // 2
B_DK = B_KEY_WIDTH // B_HEADS
B_DV = B_VAL_WIDTH // B_HEADS
B_GATE_RANK = 16
B_GATE_TAU = 16.0
B_CHUNK = 64
EVEN_IN = 2 * A_WIDTH + 2 * B_KEY_WIDTH + 2 * B_VAL_WIDTH + B_GATE_RANK
EVEN_SPLITS = (A_WIDTH, 2 * A_WIDTH, 2 * A_WIDTH + B_KEY_WIDTH, 2 * A_WIDTH + 2 * B_KEY_WIDTH,
               2 * A_WIDTH + 2 * B_KEY_WIDTH + B_VAL_WIDTH, 2 * A_WIDTH + 2 * B_KEY_WIDTH + 2 * B_VAL_WIDTH)
EVEN_OUT = A_WIDTH + B_VAL_WIDTH
C_WIDTH = D_MODEL
C_CONV = 3
D_FF = 4 * D_MODEL
EPS = 1e-6

kernel_name = "hybrid_chunkmlp_gla_shortconv_decoder_step"


def rmsnorm(x, g):
    x32 = x.astype(jnp.float32)
    y = x32 * lax.rsqrt(jnp.mean(x32 * x32, axis=-1, keepdims=True) + EPS)
    return y.astype(x.dtype) * g.astype(x.dtype)


def chunk_mlp(u, v, w_s, b_s):
    n, t, _ = u.shape
    L = min(A_CHUNK, t)
    nc = t // L
    mask = jnp.tril(jnp.ones((L, L), dtype=bool))
    w = jnp.where(mask, w_s[:, :L, :L], jnp.zeros((), w_s.dtype))
    vr = v.reshape(n, nc, L, A_HEADS, A_HEAD_DIM)
    mixed = jnp.einsum('hts,ncshd->ncthd', w, vr) + jnp.transpose(b_s[:, :L])[None, None, :, :, None]
    return (u.reshape(n, nc, L, A_HEADS, A_HEAD_DIM) * mixed).reshape(n, t, A_WIDTH)


def gla(q, k, v, log_a, s0):
    n, t = q.shape[:2]
    L = min(B_CHUNK, t)
    nc = t // L
    mask = jnp.tril(jnp.ones((L, L), dtype=bool))

    def to_chunks(a):
        return jnp.moveaxis(a.reshape(n, nc, L, *a.shape[2:]), 1, 0)

    def step(S, inp):
        qc, kc, vc, gc = inp
        G = jnp.cumsum(gc, axis=1)
        G_last = G[:, -1]
        q_in = qc * jnp.exp(G)
        k_in = kc * jnp.exp(-G)
        scores = jnp.where(mask, jnp.einsum('nthd,nshd->nhts', q_in, k_in), 0.0)
        o = jnp.einsum('nhts,nshe->nthe', scores, vc) + jnp.einsum('nthd,nhde->nthe', q_in, S)
        k_dec = kc * jnp.exp(G_last[:, None] - G)
        S = S * jnp.exp(G_last)[..., None] + jnp.einsum('nshd,nshe->nhde', k_dec, vc)
        return S, o

    S, o = lax.scan(step, s0, (to_chunks(q), to_chunks(k), to_chunks(v), to_chunks(log_a)))
    o = jnp.moveaxis(o, 0, 1).reshape(n, t, q.shape[2], v.shape[-1])
    return o, S


def even_mixer(h, w_in, w_gate_up, b_gate, w_s, b_s, g_out, w_out, s0):
    n, t, _ = h.shape
    proj = h @ w_in
    u, v, q, k, vb, r, glr = jnp.split(proj, EVEN_SPLITS, axis=-1)
    u = jax.nn.gelu(u)
    v = jax.nn.gelu(v)
    a_out = chunk_mlp(u, v, w_s, b_s)
    f32 = jnp.float32
    log_a = jax.nn.log_sigmoid((glr @ w_gate_up + b_gate).astype(f32)) / B_GATE_TAU
    qh = q.astype(f32).reshape(n, t, B_HEADS, B_DK) * (B_DK ** -0.5)
    kh = k.astype(f32).reshape(n, t, B_HEADS, B_DK)
    vh = vb.astype(f32).reshape(n, t, B_HEADS, B_DV)
    o, S = gla(qh, kh, vh, log_a.reshape(n, t, B_HEADS, B_DK), s0.astype(f32))
    o = rmsnorm(o.astype(h.dtype), g_out.reshape(B_HEADS, B_DV))
    b_out = (o * jax.nn.silu(r.reshape(n, t, B_HEADS, B_DV))).reshape(n, t, B_VAL_WIDTH)
    y = jnp.concatenate([a_out, b_out], axis=-1) @ w_out
    return y, v, S


def conv_mixer(h, w_in, conv_w, w_out, buf):
    t = h.shape[1]
    bg, cg, hx = jnp.split(h @ w_in, 3, axis=-1)
    z = cg * hx
    zp = jnp.concatenate([buf.astype(z.dtype), z], axis=1)
    conv = zp[:, 0:t] * conv_w[0]
    for j in range(1, C_CONV):
        conv = conv + zp[:, j:j + t] * conv_w[j]
    y = (bg * conv) @ w_out
    return y, zp[:, -(C_CONV - 1):]


def trunk(x, gla_states, conv_bufs, norm_mix, norm_ffn, norm_final, w_in_even, w_gate_up, b_gate,
          w_spatial, b_spatial, g_gla_out, w_out_even, w_in_odd, conv_w, w_out_odd, w_ffn_up, w_ffn_down):
    v_rows, new_gla, new_conv = [], [], []
    for l in range(DEPTH):
        i = l // 2
        h = rmsnorm(x, norm_mix[l])
        if l % 2 == 0:
            y, v, S = even_mixer(h, w_in_even[i], w_gate_up[i], b_gate[i], w_spatial[i], b_spatial[i],
                                 g_gla_out[i], w_out_even[i], gla_states[i])
            v_rows.append(v)
            new_gla.append(S.astype(x.dtype))
        else:
            y, nb = conv_mixer(h, w_in_odd[i], conv_w[i], w_out_odd[i], conv_bufs[i])
            new_conv.append(nb)
        x = x + y
        h = rmsnorm(x, norm_ffn[l])
        x = x + jnp.square(jax.nn.relu(h @ w_ffn_up[l])) @ w_ffn_down[l]
    return rmsnorm(x, norm_final), v_rows, new_gla, new_conv


def setup_inputs(seed: int = 0) -> dict:
    key = jax.random.key(seed)
    ks = jax.random.split(key, 20)
    nrm = jax.random.normal
    f32 = jnp.float32
    return {
        "x_prompt": nrm(ks[0], (BATCH, SEQ, D_MODEL), f32),
        "x_sample": nrm(ks[1], (DEC_BATCH, DEC_SEQ, D_MODEL), f32),
        "state_gla": nrm(ks[2], (N_EVEN, DEC_BATCH, B_HEADS, B_DK, B_DV), f32),
        "state_conv": nrm(ks[3], (N_ODD, DEC_BATCH, C_CONV - 1, C_WIDTH), f32),
        "norm_mix": 1.0 + 0.02 * nrm(ks[4], (DEPTH, D_MODEL), f32),
        "norm_ffn": 1.0 + 0.02 * nrm(ks[5], (DEPTH, D_MODEL), f32),
        "norm_final": 1.0 + 0.02 * nrm(ks[6], (D_MODEL,), f32),
        "w_in_even": nrm(ks[7], (N_EVEN, D_MODEL, EVEN_IN), f32) * D_MODEL ** -0.5,
        "w_gate_up": nrm(ks[8], (N_EVEN, B_GATE_RANK, B_KEY_WIDTH), f32) * B_GATE_RANK ** -0.5,
        "b_gate": 0.01 * nrm(ks[9], (N_EVEN, B_KEY_WIDTH), f32),
        "w_spatial": nrm(ks[10], (N_EVEN, A_HEADS, A_CHUNK, A_CHUNK), f32) * A_CHUNK ** -0.5,
        "b_spatial": 1.0 + 0.02 * nrm(ks[11], (N_EVEN, A_HEADS, A_CHUNK), f32),
        "g_gla_out": 1.0 + 0.02 * nrm(ks[12], (N_EVEN, B_VAL_WIDTH), f32),
        "w_out_even": nrm(ks[13], (N_EVEN, EVEN_OUT, D_MODEL), f32) * EVEN_OUT ** -0.5,
        "w_in_odd": nrm(ks[14], (N_ODD, D_MODEL, 3 * C_WIDTH), f32) * D_MODEL ** -0.5,
        "conv_w": nrm(ks[15], (N_ODD, C_CONV, C_WIDTH), f32) * C_CONV ** -0.5,
        "w_out_odd": nrm(ks[16], (N_ODD, C_WIDTH, D_MODEL), f32) * C_WIDTH ** -0.5,
        "w_ffn_up": nrm(ks[17], (DEPTH, D_MODEL, D_FF), f32) * D_MODEL ** -0.5,
        "w_ffn_down": nrm(ks[18], (DEPTH, D_FF, D_MODEL), f32) * (0.5 * D_FF ** -0.5),
    }


def reference(x_prompt, x_sample, state_gla, state_conv, norm_mix, norm_ffn, norm_final, w_in_even,
              w_gate_up, b_gate, w_spatial, b_spatial, g_gla_out, w_out_even, w_in_odd, conv_w,
              w_out_odd, w_ffn_up, w_ffn_down):
    params = (norm_mix, norm_ffn, norm_final, w_in_even, w_gate_up, b_gate, w_spatial, b_spatial,
              g_gla_out, w_out_even, w_in_odd, conv_w, w_out_odd, w_ffn_up, w_ffn_down)
    n_p = x_prompt.shape[0]
    gla0 = [jnp.zeros((n_p, B_HEADS, B_DK, B_DV), jnp.float32) for _ in range(N_EVEN)]
    conv0 = [jnp.zeros((n_p, C_CONV - 1, C_WIDTH), x_prompt.dtype) for _ in range(N_ODD)]
    y_prompt, _, gla_p, conv_p = trunk(x_prompt, gla0, conv0, *params)
    gla_s_in = [state_gla[i] for i in range(N_EVEN)]
    conv_s_in = [state_conv[i] for i in range(N_ODD)]
    y_sample, v_s, gla_s, conv_s = trunk(x_sample, gla_s_in, conv_s_in, *params)
    gla_state_prompt = jnp.stack(gla_p)
    gla_state_sample = jnp.stack(gla_s)
    conv_state_prompt = jnp.stack(conv_p)
    conv_state_sample = jnp.stack(conv_s)
    chunk_v_sample = jnp.stack(v_s)
    return (y_prompt, y_sample, gla_state_prompt, gla_state_sample, conv_state_prompt, conv_state_sample, chunk_v_sample)
```

```python
import functools

import jax
import jax.numpy as jnp
from jax import lax
from jax.experimental import pallas as pl
from jax.experimental.pallas import tpu as pltpu

_F32 = jnp.float32
_BF16 = jnp.bfloat16

_EPS = 1e-6
_HEADS = 4
_A_HEAD = 256
_A_CHUNK = 128
_DK = 128
_DV = 256
_GATE_RANK = 16
_GATE_TAU = 16.0
_GLA_ROWS = 64
_CONV_TAPS = 3

_ROW_BLOCK = 1088
_COL_TILE = 512
_VMEM_LIMIT = 56 * 1024 * 1024


def _params(semantics):
    return pltpu.CompilerParams(dimension_semantics=semantics, vmem_limit_bytes=_VMEM_LIMIT)


def _rmsnorm_rows(x, g):
    ms = jnp.mean(x * x, axis=-1, keepdims=True)
    return (x * lax.rsqrt(ms + _EPS)) * g


def _gelu_tanh(x):
    return x * (0.5 * (1.0 + jnp.tanh(0.7978845608028654 * (x + 0.044715 * (x * x * x)))))


def _log_sigmoid(x):
    return jnp.minimum(x, 0.0) - jnp.log(1.0 + jnp.exp(-jnp.abs(x)))


def _dot(a, b):
    return jnp.dot(a, b, preferred_element_type=_F32)


def _dot_nt(a, b):
    return lax.dot_general(a, b, (((1,), (1,)), ((), ())), preferred_element_type=_F32)


def _dot_tn(a, b):
    return lax.dot_general(a, b, (((0,), (0,)), ((), ())), preferred_element_type=_F32)


def _inproj_even_kernel(x_ref, g_ref, w_ref, wglr_ref, wgate_ref, bgate_ref,
                        proj_ref, v32_ref, loga_ref, hn_ref, *, gelu_tiles, v_tiles):
    j = pl.program_id(1)

    @pl.when(j == 0)
    def _():
        hn = _rmsnorm_rows(x_ref[...], g_ref[...]).astype(_BF16)
        hn_ref[...] = hn
        glr = _dot(hn, wglr_ref[...])
        gate = _dot(glr.astype(_BF16), wgate_ref[...]) + bgate_ref[...]
        loga_ref[...] = _log_sigmoid(gate) * (1.0 / _GATE_TAU)

    acc = _dot(hn_ref[...], w_ref[...])

    @pl.when(j < gelu_tiles)
    def _():
        act = _gelu_tanh(acc)
        proj_ref[...] = act.astype(_BF16)

        @pl.when(j >= v_tiles[0])
        def _():
            v32_ref[...] = act

    @pl.when(j >= gelu_tiles)
    def _():
        proj_ref[...] = acc.astype(_BF16)


def _inproj_even(x, g, w_main, w_glr, w_gate, b_gate):
    m, d = x.shape
    n = w_main.shape[1]
    bm, tn = _ROW_BLOCK, _COL_TILE
    a_width = _HEADS * _A_HEAD
    gelu_tiles = 2 * a_width // tn
    v_tiles = (a_width // tn, 2 * a_width // tn)
    nk = w_gate.shape[1]

    def v32_map(i, j):
        return (i, jnp.clip(j - v_tiles[0], 0, v_tiles[1] - v_tiles[0] - 1))

    return pl.pallas_call(
        functools.partial(_inproj_even_kernel, gelu_tiles=gelu_tiles, v_tiles=v_tiles),
        grid=(m // bm, n // tn),
        in_specs=[
            pl.BlockSpec((bm, d), lambda i, j: (i, 0)),
            pl.BlockSpec((1, d), lambda i, j: (0, 0)),
            pl.BlockSpec((d, tn), lambda i, j: (0, j)),
            pl.BlockSpec(w_glr.shape, lambda i, j: (0, 0)),
            pl.BlockSpec(w_gate.shape, lambda i, j: (0, 0)),
            pl.BlockSpec((1, nk), lambda i, j: (0, 0)),
        ],
        out_specs=[
            pl.BlockSpec((bm, tn), lambda i, j: (i, j)),
            pl.BlockSpec((bm, tn), v32_map),
            pl.BlockSpec((bm, nk), lambda i, j: (i, 0)),
        ],
        out_shape=[
            jax.ShapeDtypeStruct((m, n), _BF16),
            jax.ShapeDtypeStruct((m, a_width), _F32),
            jax.ShapeDtypeStruct((m, nk), _F32),
        ],
        scratch_shapes=[pltpu.VMEM((bm, d), _BF16)],
        compiler_params=_params(("arbitrary", "arbitrary")),
        name="inproj_even",
    )(x, g, w_main, w_glr, w_gate, b_gate)


def _inproj_odd_kernel(x_ref, g_ref, wb_ref, wc_ref, wh_ref, bg_ref, z_ref, hn_ref):
    @pl.when(pl.program_id(1) == 0)
    def _():
        hn_ref[...] = _rmsnorm_rows(x_ref[...], g_ref[...]).astype(_BF16)

    hn = hn_ref[...]
    bg_ref[...] = _dot(hn, wb_ref[...]).astype(_BF16)
    z_ref[...] = _dot(hn, wc_ref[...]) * _dot(hn, wh_ref[...])


def _inproj_odd(x, g, w):
    m, d = x.shape
    c = w.shape[1] // 3
    bm, tn = _ROW_BLOCK, _COL_TILE
    nt = c // tn
    return pl.pallas_call(
        _inproj_odd_kernel,
        grid=(m // bm, nt),
        in_specs=[
            pl.BlockSpec((bm, d), lambda i, j: (i, 0)),
            pl.BlockSpec((1, d), lambda i, j: (0, 0)),
            pl.BlockSpec((d, tn), lambda i, j: (0, j)),
            pl.BlockSpec((d, tn), lambda i, j: (0, j + nt)),
            pl.BlockSpec((d, tn), lambda i, j: (0, j + 2 * nt)),
        ],
        out_specs=[
            pl.BlockSpec((bm, tn), lambda i, j: (i, j)),
            pl.BlockSpec((bm, tn), lambda i, j: (i, j)),
        ],
        out_shape=[
            jax.ShapeDtypeStruct((m, c), _BF16),
            jax.ShapeDtypeStruct((m, c), _F32),
        ],
        scratch_shapes=[pltpu.VMEM((bm, d), _BF16)],
        compiler_params=_params(("arbitrary", "arbitrary")),
        name="inproj_odd",
    )(x, g, w, w, w)


def _outproj_kernel(a_ref, w_ref, x_ref, o_ref):
    o_ref[...] = x_ref[...] + _dot(a_ref[...], w_ref[...])


def _outproj_residual(a, w, x):
    m, k = a.shape
    n = w.shape[1]
    bm, tn = _ROW_BLOCK, _COL_TILE
    return pl.pallas_call(
        _outproj_kernel,
        grid=(m // bm, n // tn),
        in_specs=[
            pl.BlockSpec((bm, k), lambda i, j: (i, 0)),
            pl.BlockSpec((k, tn), lambda i, j: (0, j)),
            pl.BlockSpec((bm, tn), lambda i, j: (i, j)),
        ],
        out_specs=pl.BlockSpec((bm, tn), lambda i, j: (i, j)),
        out_shape=jax.ShapeDtypeStruct((m, n), _F32),
        compiler_params=_params(("arbitrary", "arbitrary")),
        name="outproj_residual",
    )(a, w, x)


def _ffn_kernel(x_ref, g_ref, wup_ref, wdn_ref, o_ref, hn_ref):
    @pl.when(pl.program_id(1) == 0)
    def _():
        x = x_ref[...]
        hn_ref[...] = _rmsnorm_rows(x, g_ref[...]).astype(_BF16)
        o_ref[...] = x

    h = _dot(hn_ref[...], wup_ref[...])
    h = jnp.square(jnp.maximum(h, 0.0)).astype(_BF16)
    o_ref[...] += _dot(h, wdn_ref[...])


def _ffn_residual(x, g, w_up, w_down):
    m, d = x.shape
    f = w_up.shape[1]
    bm, tf = _ROW_BLOCK, _COL_TILE
    return pl.pallas_call(
        _ffn_kernel,
        grid=(m // bm, f // tf),
        in_specs=[
            pl.BlockSpec((bm, d), lambda i, j: (i, 0)),
            pl.BlockSpec((1, d), lambda i, j: (0, 0)),
            pl.BlockSpec((d, tf), lambda i, j: (0, j)),
            pl.BlockSpec((tf, d), lambda i, j: (j, 0)),
        ],
        out_specs=pl.BlockSpec((bm, d), lambda i, j: (i, 0)),
        out_shape=jax.ShapeDtypeStruct((m, d), _F32),
        scratch_shapes=[pltpu.VMEM((bm, d), _BF16)],
        compiler_params=_params(("arbitrary", "arbitrary")),
        name="ffn_residual",
    )(x, g, w_up, w_down)


def _final_norm_kernel(x_ref, g_ref, o_ref):
    o_ref[...] = _rmsnorm_rows(x_ref[...], g_ref[...])


def _final_norm(x, g, row_start, rows):
    d = x.shape[1]
    br = 512
    first = row_start // br
    return pl.pallas_call(
        _final_norm_kernel,
        grid=(rows // br,),
        in_specs=[
            pl.BlockSpec((br, d), lambda i: (i + first, 0)),
            pl.BlockSpec((1, d), lambda i: (0, 0)),
        ],
        out_specs=pl.BlockSpec((br, d), lambda i: (i, 0)),
        out_shape=jax.ShapeDtypeStruct((rows, d), _F32),
        compiler_params=_params(("arbitrary",)),
        name="final_norm",
    )(x, g)


def _split3(x):
    hi = x.astype(_BF16)
    r1 = x - hi.astype(_F32)
    mid = r1.astype(_BF16)
    lo = (r1 - mid.astype(_F32)).astype(_BF16)
    return hi, mid, lo


def _mixer_even_kernel(*refs, rows, a_chunk, a_seq, seqs_per_group, carry_state):
    (u_ref, v_ref, q_ref, k_ref, vb_ref, r_ref, la_ref, wmix_ref, bcol_ref, gout_ref) = refs[:10]
    rest = refs[10:]
    if carry_state:
        mix_ref, sout_ref = rest[-2:]
        sin_ref = None
    else:
        sin_ref = rest[0]
        mix_ref, sout_ref = rest[-2:]

    tt = lax.broadcasted_iota(jnp.int32, (a_chunk, a_chunk), 0)
    ss = lax.broadcasted_iota(jnp.int32, (a_chunk, a_chunk), 1)
    a_mask = (ss <= tt) & ((tt // a_seq) == (ss // a_seq))
    for h in range(_HEADS):
        cols = slice(h * _A_HEAD, (h + 1) * _A_HEAD)
        wm = jnp.where(a_mask, wmix_ref[h], 0.0).astype(_BF16)
        bias = bcol_ref[h]
        for c in range(rows // a_chunk):
            rws = slice(c * a_chunk, (c + 1) * a_chunk)
            mixed = _dot(wm, v_ref[rws, cols]) + bias
            mix_ref[rws, cols] = (u_ref[rws, cols].astype(_F32) * mixed).astype(_BF16)

    gr = _GLA_ROWS
    seq_rows = gr // seqs_per_group
    ti = lax.broadcasted_iota(jnp.int32, (gr, gr), 0)
    si = lax.broadcasted_iota(jnp.int32, (gr, gr), 1)
    same_seq = (ti // seq_rows) == (si // seq_rows)
    causal = (si <= ti) & same_seq
    cum_and_total = jnp.concatenate(
        [jnp.where(causal, 1.0, 0.0), jnp.where(same_seq, 1.0, 0.0)], axis=0).astype(_BF16)
    row_k = lax.broadcasted_iota(jnp.int32, (gr, _DK), 0)
    row_v = lax.broadcasted_iota(jnp.int32, (gr, _DV), 0)
    pos_k = row_k % seq_rows
    ones_v = jnp.ones((gr, _DV), _BF16)
    scale = _DK ** -0.5
    a_width = _HEADS * _A_HEAD

    if carry_state:
        @pl.when(pl.program_id(1) == 0)
        def _():
            sout_ref[...] = jnp.zeros(sout_ref.shape, _F32)

    for h in range(_HEADS):
        kc = slice(h * _DK, (h + 1) * _DK)
        vc = slice(h * _DV, (h + 1) * _DV)
        g_head = gout_ref[:, vc]
        state = sout_ref[h] if carry_state else None
        for gi in range(rows // gr):
            rws = slice(gi * gr, (gi + 1) * gr)
            la = la_ref[rws, kc]
            la_hi = la.astype(_BF16)
            la_lo = (la - la_hi.astype(_F32)).astype(_BF16)
            sums = _dot(cum_and_total, jnp.concatenate([la_hi, la_lo], axis=1))
            g_cum = sums[:gr, :_DK] + sums[:gr, _DK:]
            g_tot = sums[gr:, :_DK] + sums[gr:, _DK:]
            q = q_ref[rws, kc].astype(_F32) * scale
            k = k_ref[rws, kc].astype(_F32)
            v = vb_ref[rws, vc]
            q_in = (q * jnp.exp(g_cum)).astype(_BF16)
            k_in = (k * jnp.exp(-g_cum)).astype(_BF16)
            k_dec = (k * jnp.exp(g_tot - g_cum)).astype(_BF16)
            scores = jnp.where(causal, _dot_nt(q_in, k_in), 0.0).astype(_BF16)
            o = _dot(scores, v)
            t_hi, t_mid, t_lo = _split3(g_tot)
            zero = jnp.zeros_like(t_hi)
            pieces = jnp.where(pos_k == 0, t_hi,
                               jnp.where(pos_k == 1, t_mid,
                                         jnp.where(pos_k == 2, t_lo, zero)))
            if carry_state:
                o = o + _dot(q_in, state.astype(_BF16))
                decay = jnp.exp(_dot_tn(pieces, ones_v))
                state = state * decay + _dot_tn(k_dec, v)
            else:
                for s in range(seqs_per_group):
                    in_seq_k = (row_k // seq_rows) == s
                    in_seq_v = (row_v // seq_rows) == s
                    s0 = sin_ref[s, h]
                    o = o + jnp.where(in_seq_v, _dot(q_in, s0.astype(_BF16)), 0.0)
                    decay = jnp.exp(_dot_tn(jnp.where(in_seq_k, pieces, zero), ones_v))
                    kd = jnp.where(in_seq_k, k_dec, jnp.zeros_like(k_dec))
                    sout_ref[s, h] = s0 * decay + _dot_tn(kd, v)
            ms = jnp.mean(o * o, axis=-1, keepdims=True)
            on = (o * lax.rsqrt(ms + _EPS)) * g_head
            r = r_ref[rws, vc].astype(_F32)
            gated = on * (r * (1.0 / (1.0 + jnp.exp(-r))))
            mix_ref[rws, a_width + h * _DV:a_width + (h + 1) * _DV] = gated.astype(_BF16)
        if carry_state:
            sout_ref[h] = state


def _mixer_even(proj, loga, wmix, bcol, gout, *, layer_slot, n_slots, total_rows,
                row_start, n_seqs, seq_len, state_in=None, mix_prev=None, state_prev=None):
    carry = state_in is None
    a_width = _HEADS * _A_HEAD
    kw = _HEADS * _DK
    if carry:
        rows = 256
        blocks_per_seq = seq_len // rows
        grid = (n_seqs, blocks_per_seq)
        first = row_start // rows
        rmap = lambda col: (lambda n, c: (first + n * blocks_per_seq + c, col))
        a_chunk, a_seq, spg = _A_CHUNK, _A_CHUNK, 1
        state_shape = (n_slots, n_seqs, _HEADS, _DK, _DV)
        state_spec = pl.BlockSpec((None, None, _HEADS, _DK, _DV),
                                  lambda n, c: (layer_slot, n, 0, 0, 0))
        sem = ("arbitrary", "arbitrary")
    else:
        rows = _GLA_ROWS
        spg = rows // seq_len
        grid = (n_seqs // spg,)
        first = row_start // rows
        rmap = lambda col: (lambda g: (first + g, col))
        a_chunk, a_seq = rows, seq_len
        state_shape = (n_slots, n_seqs, _HEADS, _DK, _DV)
        state_spec = pl.BlockSpec((None, spg, _HEADS, _DK, _DV),
                                  lambda g: (layer_slot, g, 0, 0, 0))
        sem = ("arbitrary",)
    const = (lambda *_: (0, 0, 0))
    in_specs = [
        pl.BlockSpec((rows, a_width), rmap(0)),
        pl.BlockSpec((rows, a_width), rmap(1)),
        pl.BlockSpec((rows, kw), rmap(2 * a_width // kw)),
        pl.BlockSpec((rows, kw), rmap(2 * a_width // kw + 1)),
        pl.BlockSpec((rows, a_width), rmap(3)),
        pl.BlockSpec((rows, a_width), rmap(4)),
        pl.BlockSpec((rows, kw), rmap(0)),
        pl.BlockSpec(wmix.shape, const),
        pl.BlockSpec(bcol.shape, const),
        pl.BlockSpec(gout.shape, lambda *_: (0, 0)),
    ]
    args = [proj, proj, proj, proj, proj, proj, loga, wmix, bcol, gout]
    if not carry:
        in_specs.append(state_spec)
        args.append(state_in)
    aliases = {}
    if mix_prev is not None:
        aliases[len(args)] = 0
        in_specs.append(pl.BlockSpec(memory_space=pl.ANY))
        args.append(mix_prev)
    if state_prev is not None:
        aliases[len(args)] = 1
        in_specs.append(pl.BlockSpec(memory_space=pl.ANY))
        args.append(state_prev)
    return pl.pallas_call(
        functools.partial(_mixer_even_kernel, rows=rows, a_chunk=a_chunk, a_seq=a_seq,
                          seqs_per_group=spg, carry_state=carry),
        grid=grid,
        in_specs=in_specs,
        out_specs=[pl.BlockSpec((rows, 2 * a_width), rmap(0)), state_spec],
        out_shape=[jax.ShapeDtypeStruct((total_rows, 2 * a_width), _BF16),
                   jax.ShapeDtypeStruct(state_shape, _F32)],
        input_output_aliases=aliases,
        compiler_params=_params(sem),
        name="mixer_even_long" if carry else "mixer_even_short",
    )(*args)


def _conv_gate_kernel(*refs, rows, seq_len, has_halo, has_hist):
    refs = list(refs)
    z_ref = refs.pop(0)
    halo_ref = refs.pop(0) if has_halo else None
    bg_ref = refs.pop(0)
    cw_ref = refs.pop(0)
    h1_ref = refs.pop(0) if has_hist else None
    h2_ref = refs.pop(0) if has_hist else None
    out_ref, zs_ref = refs[-2:]
    cols = z_ref.shape[1]

    z = z_ref[...]
    zs_ref[8:8 + rows, :] = z
    zs_ref[0:8, :] = halo_ref[...] if has_halo else jnp.zeros((8, cols), _F32)
    prev1 = zs_ref[pl.ds(7, rows), :]
    prev2 = zs_ref[pl.ds(6, rows), :]
    t = (pl.program_id(0) * rows + lax.broadcasted_iota(jnp.int32, (rows, cols), 0)) % seq_len
    hist1 = h1_ref[...] if has_hist else 0.0
    hist2 = h2_ref[...] if has_hist else 0.0
    prev1 = jnp.where(t >= 1, prev1, hist1)
    prev2 = jnp.where(t >= 2, prev2, hist2)
    conv = prev2 * cw_ref[0:1, :] + prev1 * cw_ref[1:2, :] + z * cw_ref[2:3, :]
    out_ref[...] = (bg_ref[...].astype(_F32) * conv).astype(_BF16)


def _conv_gate(z, bg, cw, *, row_start, n_rows, seq_len, hist=None, prev=None):
    total_rows, c = z.shape
    rows, tn = 512, 1024
    first = row_start // rows
    has_halo = seq_len > rows
    has_hist = hist is not None
    main = lambda i, j: (first + i, j)
    in_specs = [pl.BlockSpec((rows, tn), main)]
    args = [z]
    if has_halo:
        in_specs.append(pl.BlockSpec(
            (8, tn), lambda i, j: (jnp.maximum((first + i) * (rows // 8) - 1, 0), j)))
        args.append(z)
    in_specs += [pl.BlockSpec((rows, tn), main), pl.BlockSpec((_CONV_TAPS, tn), lambda i, j: (0, j))]
    args += [bg, cw]
    if has_hist:
        in_specs += [pl.BlockSpec((rows, tn), lambda i, j: (i, j))] * 2
        args += list(hist)
    aliases = {}
    if prev is not None:
        aliases[len(args)] = 0
        in_specs.append(pl.BlockSpec(memory_space=pl.ANY))
        args.append(prev)
    return pl.pallas_call(
        functools.partial(_conv_gate_kernel, rows=rows, seq_len=seq_len,
                          has_halo=has_halo, has_hist=has_hist),
        grid=(n_rows // rows, c // tn),
        in_specs=in_specs,
        out_specs=pl.BlockSpec((rows, tn), main),
        out_shape=jax.ShapeDtypeStruct((total_rows, c), _BF16),
        scratch_shapes=[pltpu.VMEM((rows + 8, tn), _F32)],
        input_output_aliases=aliases,
        compiler_params=_params(("arbitrary", "arbitrary")),
        name="conv_gate_long" if has_halo else "conv_gate_short",
    )(*args)


def kernel(x_prompt, x_sample, state_gla, state_conv, norm_mix, norm_ffn, norm_final, w_in_even, w_gate_up, b_gate, w_spatial, b_spatial, g_gla_out, w_out_even, w_in_odd, conv_w, w_out_odd, w_ffn_up, w_ffn_down):
    n_p, t_p, d = x_prompt.shape
    n_s, t_s, _ = x_sample.shape
    rows_p, rows_s = n_p * t_p, n_s * t_s
    total = rows_p + rows_s
    depth = norm_mix.shape[0]
    n_even = w_in_even.shape[0]
    a_width = _HEADS * _A_HEAD
    main_cols = w_in_even.shape[2] - _GATE_RANK

    x = jnp.concatenate([x_prompt.reshape(rows_p, d), x_sample.reshape(rows_s, d)], axis=0)

    reps = _GLA_ROWS // t_s
    wmix_s = jnp.tile(w_spatial[:, :, :t_s, :t_s], (1, 1, reps, reps))
    bcol_s = jnp.tile(b_spatial[:, :, :t_s], (1, 1, reps))[..., None]
    bcol_p = b_spatial[..., None]

    v_rows, conv_p, conv_s = [], [], []
    gla_p = gla_s = None
    for l in range(depth):
        i = l // 2
        g_mix = norm_mix[l][None, :]
        if l % 2 == 0:
            w_main = w_in_even[i, :, :main_cols].astype(_BF16)
            w_glr = jnp.pad(w_in_even[i, :, main_cols:], ((0, 0), (0, 128 - _GATE_RANK))).astype(_BF16)
            w_gate = jnp.pad(w_gate_up[i], ((0, 128 - _GATE_RANK), (0, 0))).astype(_BF16)
            proj, v32, loga = _inproj_even(x, g_mix, w_main, w_glr, w_gate, b_gate[i][None, :])
            gout = g_gla_out[i][None, :]
            mix, gla_p = _mixer_even(
                proj, loga, w_spatial[i], bcol_p[i], gout, layer_slot=i, n_slots=n_even,
                total_rows=total, row_start=0, n_seqs=n_p, seq_len=t_p, state_prev=gla_p)
            mix, gla_s = _mixer_even(
                proj, loga, wmix_s[i], bcol_s[i], gout, layer_slot=i, n_slots=n_even,
                total_rows=total, row_start=rows_p, n_seqs=n_s, seq_len=t_s,
                state_in=state_gla, mix_prev=mix, state_prev=gla_s)
            v_rows.append(v32[rows_p:].reshape(n_s, t_s, a_width))
            x = _outproj_residual(mix, w_out_even[i].astype(_BF16), x)
        else:
            bg, z = _inproj_odd(x, g_mix, w_in_odd[i].astype(_BF16))
            buf = state_conv[i].astype(_F32)
            zero_hist = jnp.zeros((n_s, t_s, d), _F32)
            hist1 = zero_hist.at[:, 0].set(buf[:, 1]).reshape(rows_s, d)
            hist2 = zero_hist.at[:, 0].set(buf[:, 0]).at[:, 1].set(buf[:, 1]).reshape(rows_s, d)
            gated = _conv_gate(z, bg, conv_w[i], row_start=0, n_rows=rows_p, seq_len=t_p)
            gated = _conv_gate(z, bg, conv_w[i], row_start=rows_p, n_rows=rows_s, seq_len=t_s,
                               hist=(hist1, hist2), prev=gated)
            conv_p.append(z[:rows_p].reshape(n_p, t_p, d)[:, t_p - (_CONV_TAPS - 1):])
            conv_s.append(z[rows_p:].reshape(n_s, t_s, d)[:, t_s - (_CONV_TAPS - 1):])
            x = _outproj_residual(gated, w_out_odd[i].astype(_BF16), x)
        x = _ffn_residual(x, norm_ffn[l][None, :], w_ffn_up[l].astype(_BF16), w_ffn_down[l].astype(_BF16))

    g_fin = norm_final[None, :]
    y_prompt = _final_norm(x, g_fin, 0, rows_p).reshape(n_p, t_p, d)
    y_sample = _final_norm(x, g_fin, rows_p, rows_s).reshape(n_s, t_s, d)
    return (y_prompt, y_sample, gla_p, gla_s, jnp.stack(conv_p), jnp.stack(conv_s), jnp.stack(v_rows))
```

```python
import functools

import jax
import jax.numpy as jnp
from jax import lax
from jax.experimental import pallas as pl
from jax.experimental.pallas import tpu as pltpu

_F32 = jnp.float32
_BF16 = jnp.bfloat16

_EPS = 1e-6
_HEADS = 4
_A_HEAD = 256
_A_CHUNK = 128
_DK = 128
_DV = 256
_GATE_RANK = 16
_GATE_TAU = 16.0
_GLA_ROWS = 64
_CONV_TAPS = 3

_ROW_BLOCK = 1088
_COL_TILE = 512
_VMEM_LIMIT = 56 * 1024 * 1024


def _params(semantics):
    return pltpu.CompilerParams(dimension_semantics=semantics, vmem_limit_bytes=_VMEM_LIMIT)


def _rmsnorm_rows(x, g):
    ms = jnp.mean(x * x, axis=-1, keepdims=True)
    return (x * lax.rsqrt(ms + _EPS)) * g


def _gelu_tanh(x):
    return x * (0.5 * (1.0 + jnp.tanh(0.7978845608028654 * (x + 0.044715 * (x * x * x)))))


def _log_sigmoid(x):
    return jnp.minimum(x, 0.0) - jnp.log(1.0 + jnp.exp(-jnp.abs(x)))


def _dot(a, b):
    return jnp.dot(a, b, preferred_element_type=_F32)


def _dot_nt(a, b):
    return lax.dot_general(a, b, (((1,), (1,)), ((), ())), preferred_element_type=_F32)


def _dot_tn(a, b):
    return lax.dot_general(a, b, (((0,), (0,)), ((), ())), preferred_element_type=_F32)


def _inproj_even_kernel(x_ref, g_ref, w_ref, wglr_ref, wgate_ref, bgate_ref,
                        proj_ref, v32_ref, loga_ref, hn_ref, *, gelu_tiles, v_tiles):
    j = pl.program_id(1)

    @pl.when(j == 0)
    def _():
        hn = _rmsnorm_rows(x_ref[...], g_ref[...]).astype(_BF16)
        hn_ref[...] = hn
        glr = _dot(hn, wglr_ref[...])
        gate = _dot(glr.astype(_BF16), wgate_ref[...]) + bgate_ref[...]
        loga_ref[...] = _log_sigmoid(gate) * (1.0 / _GATE_TAU)

    acc = _dot(hn_ref[...], w_ref[...])

    @pl.when(j < gelu_tiles)
    def _():
        act = _gelu_tanh(acc)
        proj_ref[...] = act.astype(_BF16)

        @pl.when(j >= v_tiles[0])
        def _():
            v32_ref[...] = act

    @pl.when(j >= gelu_tiles)
    def _():
        proj_ref[...] = acc.astype(_BF16)


def _inproj_even(x, g, w_all, layer, w_glr, w_gate, b_gate):
    m, d = x.shape
    n = w_all.shape[2] - _GATE_RANK
    bm, tn = _ROW_BLOCK, _COL_TILE
    a_width = _HEADS * _A_HEAD
    gelu_tiles = 2 * a_width // tn
    v_tiles = (a_width // tn, 2 * a_width // tn)
    nk = w_gate.shape[1]

    def v32_map(i, j):
        return (i, jnp.clip(j - v_tiles[0], 0, v_tiles[1] - v_tiles[0] - 1))

    return pl.pallas_call(
        functools.partial(_inproj_even_kernel, gelu_tiles=gelu_tiles, v_tiles=v_tiles),
        grid=(m // bm, n // tn),
        in_specs=[
            pl.BlockSpec((bm, d), lambda i, j: (i, 0)),
            pl.BlockSpec((1, d), lambda i, j: (0, 0)),
            pl.BlockSpec((None, d, tn), lambda i, j: (layer, 0, j)),
            pl.BlockSpec(w_glr.shape, lambda i, j: (0, 0)),
            pl.BlockSpec(w_gate.shape, lambda i, j: (0, 0)),
            pl.BlockSpec((1, nk), lambda i, j: (0, 0)),
        ],
        out_specs=[
            pl.BlockSpec((bm, tn), lambda i, j: (i, j)),
            pl.BlockSpec((bm, tn), v32_map),
            pl.BlockSpec((bm, nk), lambda i, j: (i, 0)),
        ],
        out_shape=[
            jax.ShapeDtypeStruct((m, n), _BF16),
            jax.ShapeDtypeStruct((m, a_width), _F32),
            jax.ShapeDtypeStruct((m, nk), _F32),
        ],
        scratch_shapes=[pltpu.VMEM((bm, d), _BF16)],
        compiler_params=_params(("arbitrary", "arbitrary")),
        name="inproj_even",
    )(x, g, w_all, w_glr, w_gate, b_gate)


def _inproj_odd_kernel(x_ref, g_ref, wb_ref, wc_ref, wh_ref, bg_ref, z_ref, hn_ref):
    @pl.when(pl.program_id(1) == 0)
    def _():
        hn_ref[...] = _rmsnorm_rows(x_ref[...], g_ref[...]).astype(_BF16)

    hn = hn_ref[...]
    bg_ref[...] = _dot(hn, wb_ref[...]).astype(_BF16)
    z_ref[...] = _dot(hn, wc_ref[...]) * _dot(hn, wh_ref[...])


def _inproj_odd(x, g, w, layer):
    m, d = x.shape
    c = w.shape[2] // 3
    bm, tn = _ROW_BLOCK, _COL_TILE
    nt = c // tn
    return pl.pallas_call(
        _inproj_odd_kernel,
        grid=(m // bm, nt),
        in_specs=[
            pl.BlockSpec((bm, d), lambda i, j: (i, 0)),
            pl.BlockSpec((1, d), lambda i, j: (0, 0)),
            pl.BlockSpec((None, d, tn), lambda i, j: (layer, 0, j)),
            pl.BlockSpec((None, d, tn), lambda i, j: (layer, 0, j + nt)),
            pl.BlockSpec((None, d, tn), lambda i, j: (layer, 0, j + 2 * nt)),
        ],
        out_specs=[
            pl.BlockSpec((bm, tn), lambda i, j: (i, j)),
            pl.BlockSpec((bm, tn), lambda i, j: (i, j)),
        ],
        out_shape=[
            jax.ShapeDtypeStruct((m, c), _BF16),
            jax.ShapeDtypeStruct((m, c), _F32),
        ],
        scratch_shapes=[pltpu.VMEM((bm, d), _BF16)],
        compiler_params=_params(("arbitrary", "arbitrary")),
        name="inproj_odd",
    )(x, g, w, w, w)


def _outproj_kernel(a_ref, w_ref, x_ref, o_ref):
    o_ref[...] = x_ref[...] + _dot(a_ref[...], w_ref[...])


def _outproj_residual(a, w, layer, x):
    m, k = a.shape
    n = w.shape[2]
    bm, tn = _ROW_BLOCK, _COL_TILE
    return pl.pallas_call(
        _outproj_kernel,
        grid=(m // bm, n // tn),
        in_specs=[
            pl.BlockSpec((bm, k), lambda i, j: (i, 0)),
            pl.BlockSpec((None, k, tn), lambda i, j: (layer, 0, j)),
            pl.BlockSpec((bm, tn), lambda i, j: (i, j)),
        ],
        out_specs=pl.BlockSpec((bm, tn), lambda i, j: (i, j)),
        out_shape=jax.ShapeDtypeStruct((m, n), _F32),
        compiler_params=_params(("arbitrary", "arbitrary")),
        name="outproj_residual",
    )(a, w, x)


def _ffn_kernel(x_ref, g_ref, wup_ref, wdn_ref, o_ref, hn_ref):
    @pl.when(pl.program_id(1) == 0)
    def _():
        x = x_ref[...]
        hn_ref[...] = _rmsnorm_rows(x, g_ref[...]).astype(_BF16)
        o_ref[...] = x

    h = _dot(hn_ref[...], wup_ref[...])
    h = jnp.square(jnp.maximum(h, 0.0)).astype(_BF16)
    o_ref[...] += _dot(h, wdn_ref[...])


def _ffn_residual(x, g, w_up, w_down, layer):
    m, d = x.shape
    f = w_up.shape[2]
    bm, tf = _ROW_BLOCK, _COL_TILE
    return pl.pallas_call(
        _ffn_kernel,
        grid=(m // bm, f // tf),
        in_specs=[
            pl.BlockSpec((bm, d), lambda i, j: (i, 0)),
            pl.BlockSpec((1, d), lambda i, j: (0, 0)),
            pl.BlockSpec((None, d, tf), lambda i, j: (layer, 0, j)),
            pl.BlockSpec((None, tf, d), lambda i, j: (layer, j, 0)),
        ],
        out_specs=pl.BlockSpec((bm, d), lambda i, j: (i, 0)),
        out_shape=jax.ShapeDtypeStruct((m, d), _F32),
        scratch_shapes=[pltpu.VMEM((bm, d), _BF16)],
        compiler_params=_params(("arbitrary", "arbitrary")),
        name="ffn_residual",
    )(x, g, w_up, w_down)


def _final_norm_kernel(x_ref, g_ref, o_ref):
    o_ref[...] = _rmsnorm_rows(x_ref[...], g_ref[...])


def _final_norm(x, g, row_start, rows):
    d = x.shape[1]
    br = 512
    first = row_start // br
    return pl.pallas_call(
        _final_norm_kernel,
        grid=(rows // br,),
        in_specs=[
            pl.BlockSpec((br, d), lambda i: (i + first, 0)),
            pl.BlockSpec((1, d), lambda i: (0, 0)),
        ],
        out_specs=pl.BlockSpec((br, d), lambda i: (i, 0)),
        out_shape=jax.ShapeDtypeStruct((rows, d), _F32),
        compiler_params=_params(("arbitrary",)),
        name="final_norm",
    )(x, g)


def _split3(x):
    hi = x.astype(_BF16)
    r1 = x - hi.astype(_F32)
    mid = r1.astype(_BF16)
    lo = (r1 - mid.astype(_F32)).astype(_BF16)
    return hi, mid, lo


def _mixer_even_kernel(*refs, rows, a_chunk, a_seq, seqs_per_group, carry_state):
    (u_ref, v_ref, q_ref, k_ref, vb_ref, r_ref, la_ref, wmix_ref, bcol_ref, gout_ref) = refs[:10]
    rest = refs[10:]
    if carry_state:
        mix_ref, sout_ref = rest[-2:]
        sin_ref = None
    else:
        sin_ref = rest[0]
        mix_ref, sout_ref = rest[-2:]

    tt = lax.broadcasted_iota(jnp.int32, (a_chunk, a_chunk), 0)
    ss = lax.broadcasted_iota(jnp.int32, (a_chunk, a_chunk), 1)
    a_mask = (ss <= tt) & ((tt // a_seq) == (ss // a_seq))
    for h in range(_HEADS):
        cols = slice(h * _A_HEAD, (h + 1) * _A_HEAD)
        wm = jnp.where(a_mask, wmix_ref[h], 0.0).astype(_BF16)
        bias = bcol_ref[h]
        for c in range(rows // a_chunk):
            rws = slice(c * a_chunk, (c + 1) * a_chunk)
            mixed = _dot(wm, v_ref[rws, cols]) + bias
            mix_ref[rws, cols] = (u_ref[rws, cols].astype(_F32) * mixed).astype(_BF16)

    gr = _GLA_ROWS
    seq_rows = gr // seqs_per_group
    ti = lax.broadcasted_iota(jnp.int32, (gr, gr), 0)
    si = lax.broadcasted_iota(jnp.int32, (gr, gr), 1)
    same_seq = (ti // seq_rows) == (si // seq_rows)
    causal = (si <= ti) & same_seq
    cum_and_total = jnp.concatenate(
        [jnp.where(causal, 1.0, 0.0), jnp.where(same_seq, 1.0, 0.0)], axis=0).astype(_BF16)
    row_k = lax.broadcasted_iota(jnp.int32, (gr, _DK), 0)
    row_v = lax.broadcasted_iota(jnp.int32, (gr, _DV), 0)
    pos_k = row_k % seq_rows
    ones_v = jnp.ones((gr, _DV), _BF16)
    scale = _DK ** -0.5
    a_width = _HEADS * _A_HEAD

    if carry_state:
        @pl.when(pl.program_id(1) == 0)
        def _():
            sout_ref[...] = jnp.zeros(sout_ref.shape, _F32)

    for h in range(_HEADS):
        kc = slice(h * _DK, (h + 1) * _DK)
        vc = slice(h * _DV, (h + 1) * _DV)
        g_head = gout_ref[:, vc]
        state = sout_ref[h] if carry_state else None
        for gi in range(rows // gr):
            rws = slice(gi * gr, (gi + 1) * gr)
            la = la_ref[rws, kc]
            la_hi = la.astype(_BF16)
            la_lo = (la - la_hi.astype(_F32)).astype(_BF16)
            sums = _dot(cum_and_total, jnp.concatenate([la_hi, la_lo], axis=1))
            g_cum = sums[:gr, :_DK] + sums[:gr, _DK:]
            g_tot = sums[gr:, :_DK] + sums[gr:, _DK:]
            q = q_ref[rws, kc].astype(_F32) * scale
            k = k_ref[rws, kc].astype(_F32)
            v = vb_ref[rws, vc]
            q_in = (q * jnp.exp(g_cum)).astype(_BF16)
            k_in = (k * jnp.exp(-g_cum)).astype(_BF16)
            k_dec = (k * jnp.exp(g_tot - g_cum)).astype(_BF16)
            scores = jnp.where(causal, _dot_nt(q_in, k_in), 0.0).astype(_BF16)
            o = _dot(scores, v)
            t_hi, t_mid, t_lo = _split3(g_tot)
            zero = jnp.zeros_like(t_hi)
            pieces = jnp.where(pos_k == 0, t_hi,
                               jnp.where(pos_k == 1, t_mid,
                                         jnp.where(pos_k == 2, t_lo, zero)))
            if carry_state:
                o = o + _dot(q_in, state.astype(_BF16))
                decay = jnp.exp(_dot_tn(pieces, ones_v))
                state = state * decay + _dot_tn(k_dec, v)
            else:
                for s in range(seqs_per_group):
                    in_seq_k = (row_k // seq_rows) == s
                    in_seq_v = (row_v // seq_rows) == s
                    s0 = sin_ref[s, h]
                    o = o + jnp.where(in_seq_v, _dot(q_in, s0.astype(_BF16)), 0.0)
                    decay = jnp.exp(_dot_tn(jnp.where(in_seq_k, pieces, zero), ones_v))
                    kd = jnp.where(in_seq_k, k_dec, jnp.zeros_like(k_dec))
                    sout_ref[s, h] = s0 * decay + _dot_tn(kd, v)
            ms = jnp.mean(o * o, axis=-1, keepdims=True)
            on = (o * lax.rsqrt(ms + _EPS)) * g_head
            r = r_ref[rws, vc].astype(_F32)
            gated = on * (r * (1.0 / (1.0 + jnp.exp(-r))))
            mix_ref[rws, a_width + h * _DV:a_width + (h + 1) * _DV] = gated.astype(_BF16)
        if carry_state:
            sout_ref[h] = state


def _mixer_even(proj, loga, wmix, bcol, gout, *, layer_slot, n_slots, total_rows,
                row_start, n_seqs, seq_len, state_in=None, mix_prev=None, state_prev=None):
    carry = state_in is None
    a_width = _HEADS * _A_HEAD
    kw = _HEADS * _DK
    if carry:
        rows = 256
        blocks_per_seq = seq_len // rows
        grid = (n_seqs, blocks_per_seq)
        first = row_start // rows
        rmap = lambda col: (lambda n, c: (first + n * blocks_per_seq + c, col))
        a_chunk, a_seq, spg = _A_CHUNK, _A_CHUNK, 1
        state_shape = (n_slots, n_seqs, _HEADS, _DK, _DV)
        state_spec = pl.BlockSpec((None, None, _HEADS, _DK, _DV),
                                  lambda n, c: (layer_slot, n, 0, 0, 0))
        sem = ("arbitrary", "arbitrary")
    else:
        rows = _GLA_ROWS
        spg = rows // seq_len
        grid = (n_seqs // spg,)
        first = row_start // rows
        rmap = lambda col: (lambda g: (first + g, col))
        a_chunk, a_seq = rows, seq_len
        state_shape = (n_slots, n_seqs, _HEADS, _DK, _DV)
        state_spec = pl.BlockSpec((None, spg, _HEADS, _DK, _DV),
                                  lambda g: (layer_slot, g, 0, 0, 0))
        sem = ("arbitrary",)
    const = (lambda *_: (0, 0, 0))
    in_specs = [
        pl.BlockSpec((rows, a_width), rmap(0)),
        pl.BlockSpec((rows, a_width), rmap(1)),
        pl.BlockSpec((rows, kw), rmap(2 * a_width // kw)),
        pl.BlockSpec((rows, kw), rmap(2 * a_width // kw + 1)),
        pl.BlockSpec((rows, a_width), rmap(3)),
        pl.BlockSpec((rows, a_width), rmap(4)),
        pl.BlockSpec((rows, kw), rmap(0)),
        pl.BlockSpec(wmix.shape, const),
        pl.BlockSpec(bcol.shape, const),
        pl.BlockSpec(gout.shape, lambda *_: (0, 0)),
    ]
    args = [proj, proj, proj, proj, proj, proj, loga, wmix, bcol, gout]
    if not carry:
        in_specs.append(state_spec)
        args.append(state_in)
    aliases = {}
    if mix_prev is not None:
        aliases[len(args)] = 0
        in_specs.append(pl.BlockSpec(memory_space=pl.ANY))
        args.append(mix_prev)
    if state_prev is not None:
        aliases[len(args)] = 1
        in_specs.append(pl.BlockSpec(memory_space=pl.ANY))
        args.append(state_prev)
    return pl.pallas_call(
        functools.partial(_mixer_even_kernel, rows=rows, a_chunk=a_chunk, a_seq=a_seq,
                          seqs_per_group=spg, carry_state=carry),
        grid=grid,
        in_specs=in_specs,
        out_specs=[pl.BlockSpec((rows, 2 * a_width), rmap(0)), state_spec],
        out_shape=[jax.ShapeDtypeStruct((total_rows, 2 * a_width), _BF16),
                   jax.ShapeDtypeStruct(state_shape, _F32)],
        input_output_aliases=aliases,
        compiler_params=_params(sem),
        name="mixer_even_long" if carry else "mixer_even_short",
    )(*args)


def _conv_gate_kernel(*refs, rows, seq_len, has_halo, has_hist):
    refs = list(refs)
    z_ref = refs.pop(0)
    halo_ref = refs.pop(0) if has_halo else None
    bg_ref = refs.pop(0)
    cw_ref = refs.pop(0)
    h1_ref = refs.pop(0) if has_hist else None
    h2_ref = refs.pop(0) if has_hist else None
    out_ref, zs_ref = refs[-2:]
    cols = z_ref.shape[1]

    z = z_ref[...]
    zs_ref[8:8 + rows, :] = z
    zs_ref[0:8, :] = halo_ref[...] if has_halo else jnp.zeros((8, cols), _F32)
    prev1 = zs_ref[pl.ds(7, rows), :]
    prev2 = zs_ref[pl.ds(6, rows), :]
    t = (pl.program_id(0) * rows + lax.broadcasted_iota(jnp.int32, (rows, cols), 0)) % seq_len
    hist1 = h1_ref[...] if has_hist else 0.0
    hist2 = h2_ref[...] if has_hist else 0.0
    prev1 = jnp.where(t >= 1, prev1, hist1)
    prev2 = jnp.where(t >= 2, prev2, hist2)
    conv = prev2 * cw_ref[0:1, :] + prev1 * cw_ref[1:2, :] + z * cw_ref[2:3, :]
    out_ref[...] = (bg_ref[...].astype(_F32) * conv).astype(_BF16)


def _conv_gate(z, bg, cw, *, row_start, n_rows, seq_len, hist=None, prev=None):
    total_rows, c = z.shape
    rows, tn = 512, 1024
    first = row_start // rows
    has_halo = seq_len > rows
    has_hist = hist is not None
    main = lambda i, j: (first + i, j)
    in_specs = [pl.BlockSpec((rows, tn), main)]
    args = [z]
    if has_halo:
        in_specs.append(pl.BlockSpec(
            (8, tn), lambda i, j: (jnp.maximum((first + i) * (rows // 8) - 1, 0), j)))
        args.append(z)
    in_specs += [pl.BlockSpec((rows, tn), main), pl.BlockSpec((_CONV_TAPS, tn), lambda i, j: (0, j))]
    args += [bg, cw]
    if has_hist:
        in_specs += [pl.BlockSpec((rows, tn), lambda i, j: (i, j))] * 2
        args += list(hist)
    aliases = {}
    if prev is not None:
        aliases[len(args)] = 0
        in_specs.append(pl.BlockSpec(memory_space=pl.ANY))
        args.append(prev)
    return pl.pallas_call(
        functools.partial(_conv_gate_kernel, rows=rows, seq_len=seq_len,
                          has_halo=has_halo, has_hist=has_hist),
        grid=(n_rows // rows, c // tn),
        in_specs=in_specs,
        out_specs=pl.BlockSpec((rows, tn), main),
        out_shape=jax.ShapeDtypeStruct((total_rows, c), _BF16),
        scratch_shapes=[pltpu.VMEM((rows + 8, tn), _F32)],
        input_output_aliases=aliases,
        compiler_params=_params(("arbitrary", "arbitrary")),
        name="conv_gate_long" if has_halo else "conv_gate_short",
    )(*args)


def kernel(x_prompt, x_sample, state_gla, state_conv, norm_mix, norm_ffn, norm_final, w_in_even, w_gate_up, b_gate, w_spatial, b_spatial, g_gla_out, w_out_even, w_in_odd, conv_w, w_out_odd, w_ffn_up, w_ffn_down):
    n_p, t_p, d = x_prompt.shape
    n_s, t_s, _ = x_sample.shape
    rows_p, rows_s = n_p * t_p, n_s * t_s
    total = rows_p + rows_s
    depth = norm_mix.shape[0]
    n_even = w_in_even.shape[0]
    a_width = _HEADS * _A_HEAD
    main_cols = w_in_even.shape[2] - _GATE_RANK

    x = jnp.concatenate([x_prompt.reshape(rows_p, d), x_sample.reshape(rows_s, d)], axis=0)

    reps = _GLA_ROWS // t_s
    wmix_s = jnp.tile(w_spatial[:, :, :t_s, :t_s], (1, 1, reps, reps))
    bcol_s = jnp.tile(b_spatial[:, :, :t_s], (1, 1, reps))[..., None]
    bcol_p = b_spatial[..., None]

    w_in_even_b = w_in_even.astype(_BF16)
    w_in_odd_b = w_in_odd.astype(_BF16)
    w_out_even_b = w_out_even.astype(_BF16)
    w_out_odd_b = w_out_odd.astype(_BF16)
    w_up_b = w_ffn_up.astype(_BF16)
    w_down_b = w_ffn_down.astype(_BF16)

    v_rows, conv_p, conv_s = [], [], []
    gla_p = gla_s = None
    for l in range(depth):
        i = l // 2
        g_mix = norm_mix[l][None, :]
        if l % 2 == 0:
            w_glr = jnp.pad(w_in_even[i, :, main_cols:], ((0, 0), (0, 128 - _GATE_RANK))).astype(_BF16)
            w_gate = jnp.pad(w_gate_up[i], ((0, 128 - _GATE_RANK), (0, 0))).astype(_BF16)
            proj, v32, loga = _inproj_even(x, g_mix, w_in_even_b, i, w_glr, w_gate, b_gate[i][None, :])
            gout = g_gla_out[i][None, :]
            mix, gla_p = _mixer_even(
                proj, loga, w_spatial[i], bcol_p[i], gout, layer_slot=i, n_slots=n_even,
                total_rows=total, row_start=0, n_seqs=n_p, seq_len=t_p, state_prev=gla_p)
            mix, gla_s = _mixer_even(
                proj, loga, wmix_s[i], bcol_s[i], gout, layer_slot=i, n_slots=n_even,
                total_rows=total, row_start=rows_p, n_seqs=n_s, seq_len=t_s,
                state_in=state_gla, mix_prev=mix, state_prev=gla_s)
            v_rows.append(v32[rows_p:].reshape(n_s, t_s, a_width))
            x = _outproj_residual(mix, w_out_even_b, i, x)
        else:
            bg, z = _inproj_odd(x, g_mix, w_in_odd_b, i)
            buf = state_conv[i].astype(_F32)
            keep = _CONV_TAPS - 1
            hist1 = jnp.pad(buf[:, 1:2], ((0, 0), (0, t_s - 1), (0, 0))).reshape(rows_s, d)
            hist2 = jnp.pad(buf, ((0, 0), (0, t_s - keep), (0, 0))).reshape(rows_s, d)
            gated = _conv_gate(z, bg, conv_w[i], row_start=0, n_rows=rows_p, seq_len=t_p)
            gated = _conv_gate(z, bg, conv_w[i], row_start=rows_p, n_rows=rows_s, seq_len=t_s,
                               hist=(hist1, hist2), prev=gated)
            conv_p.append(jnp.stack([z[(n + 1) * t_p - keep:(n + 1) * t_p] for n in range(n_p)]))
            conv_s.append(z[rows_p:].reshape(n_s, t_s, d)[:, t_s - keep:])
            x = _outproj_residual(gated, w_out_odd_b, i, x)
        x = _ffn_residual(x, norm_ffn[l][None, :], w_up_b, w_down_b, l)

    g_fin = norm_final[None, :]
    y_prompt = _final_norm(x, g_fin, 0, rows_p).reshape(n_p, t_p, d)
    y_sample = _final_norm(x, g_fin, rows_p, rows_s).reshape(n_s, t_s, d)
    return (y_prompt, y_sample, gla_p, gla_s, jnp.stack(conv_p), jnp.stack(conv_s), jnp.stack(v_rows))
```

```python
import functools

import jax
import jax.numpy as jnp
from jax import lax
from jax.experimental import pallas as pl
from jax.experimental.pallas import tpu as pltpu

_F32 = jnp.float32
_BF16 = jnp.bfloat16

_EPS = 1e-6
_HEADS = 4
_A_HEAD = 256
_A_CHUNK = 128
_DK = 128
_DV = 256
_GATE_RANK = 16
_GATE_TAU = 16.0
_GLA_ROWS = 64
_CONV_TAPS = 3

_ROW_BLOCK = 1088
_COL_TILE = 512
_VMEM_LIMIT = 56 * 1024 * 1024


def _params(semantics):
    return pltpu.CompilerParams(dimension_semantics=semantics, vmem_limit_bytes=_VMEM_LIMIT)


def _rmsnorm_rows(x, g):
    ms = jnp.mean(x * x, axis=-1, keepdims=True)
    return (x * lax.rsqrt(ms + _EPS)) * g


def _gelu_tanh(x):
    return x * (0.5 * (1.0 + jnp.tanh(0.7978845608028654 * (x + 0.044715 * (x * x * x)))))


def _log_sigmoid(x):
    return jnp.minimum(x, 0.0) - jnp.log(1.0 + jnp.exp(-jnp.abs(x)))


def _dot(a, b):
    return jnp.dot(a, b, preferred_element_type=_F32)


def _dot_nt(a, b):
    return lax.dot_general(a, b, (((1,), (1,)), ((), ())), preferred_element_type=_F32)


def _dot_tn(a, b):
    return lax.dot_general(a, b, (((0,), (0,)), ((), ())), preferred_element_type=_F32)


def _inproj_even_kernel(x_ref, g_ref, w_ref, wglr_ref, wgate_ref, bgate_ref,
                        proj_ref, v32_ref, loga_ref, hn_ref, *, gelu_tiles, v_tiles, tail_start):
    i, j = pl.program_id(0), pl.program_id(1)

    @pl.when(j == 0)
    def _():
        hn = _rmsnorm_rows(x_ref[...], g_ref[...]).astype(_BF16)
        hn_ref[...] = hn
        glr = _dot(hn, wglr_ref[...])
        gate = _dot(glr.astype(_BF16), wgate_ref[...]) + bgate_ref[...]
        loga_ref[...] = _log_sigmoid(gate) * (1.0 / _GATE_TAU)

    acc = _dot(hn_ref[...], w_ref[...].astype(_BF16))
    proj_ref[...] = jnp.where(j < gelu_tiles, _gelu_tanh(acc), acc).astype(_BF16)

    @pl.when((i == pl.num_programs(0) - 1) & (j >= v_tiles[0]) & (j < v_tiles[1]))
    def _():
        v32_ref[...] = _gelu_tanh(_dot(hn_ref[tail_start:, :], w_ref[...].astype(_BF16)))


def _inproj_even(x, g, w_all, layer, w_glr, w_gate, b_gate, tail_rows):
    m, d = x.shape
    n = w_all.shape[2] - _GATE_RANK
    bm, tn = _ROW_BLOCK, _COL_TILE
    assert tail_rows <= bm and (bm - tail_rows) % 16 == 0
    a_width = _HEADS * _A_HEAD
    gelu_tiles = 2 * a_width // tn
    v_tiles = (a_width // tn, 2 * a_width // tn)
    nk = w_gate.shape[1]

    def v32_map(i, j):
        tile = jnp.clip(j - v_tiles[0], 0, v_tiles[1] - v_tiles[0] - 1)
        return (0, jnp.where(i == m // bm - 1, tile, 0))

    return pl.pallas_call(
        functools.partial(_inproj_even_kernel, gelu_tiles=gelu_tiles, v_tiles=v_tiles,
                          tail_start=bm - tail_rows),
        grid=(m // bm, n // tn),
        in_specs=[
            pl.BlockSpec((bm, d), lambda i, j: (i, 0)),
            pl.BlockSpec((1, d), lambda i, j: (0, 0)),
            pl.BlockSpec((None, d, tn), lambda i, j: (layer, 0, j)),
            pl.BlockSpec(w_glr.shape, lambda i, j: (0, 0)),
            pl.BlockSpec(w_gate.shape, lambda i, j: (0, 0)),
            pl.BlockSpec((1, nk), lambda i, j: (0, 0)),
        ],
        out_specs=[
            pl.BlockSpec((bm, tn), lambda i, j: (i, j)),
            pl.BlockSpec((tail_rows, tn), v32_map),
            pl.BlockSpec((bm, nk), lambda i, j: (i, 0)),
        ],
        out_shape=[
            jax.ShapeDtypeStruct((m, n), _BF16),
            jax.ShapeDtypeStruct((tail_rows, a_width), _F32),
            jax.ShapeDtypeStruct((m, nk), _F32),
        ],
        scratch_shapes=[pltpu.VMEM((bm, d), _BF16)],
        compiler_params=_params(("arbitrary", "arbitrary")),
        name="inproj_even",
    )(x, g, w_all, w_glr, w_gate, b_gate)


def _inproj_odd_kernel(x_ref, g_ref, wb_ref, wc_ref, wh_ref, bg_ref, z_ref, hn_ref):
    @pl.when(pl.program_id(1) == 0)
    def _():
        hn_ref[...] = _rmsnorm_rows(x_ref[...], g_ref[...]).astype(_BF16)

    hn = hn_ref[...]
    bg_ref[...] = _dot(hn, wb_ref[...].astype(_BF16)).astype(_BF16)
    z_ref[...] = _dot(hn, wc_ref[...].astype(_BF16)) * _dot(hn, wh_ref[...].astype(_BF16))


def _inproj_odd(x, g, w, layer):
    m, d = x.shape
    c = w.shape[2] // 3
    bm, tn = _ROW_BLOCK, _COL_TILE // 2
    nt = c // tn
    return pl.pallas_call(
        _inproj_odd_kernel,
        grid=(m // bm, nt),
        in_specs=[
            pl.BlockSpec((bm, d), lambda i, j: (i, 0)),
            pl.BlockSpec((1, d), lambda i, j: (0, 0)),
            pl.BlockSpec((None, d, tn), lambda i, j: (layer, 0, j)),
            pl.BlockSpec((None, d, tn), lambda i, j: (layer, 0, j + nt)),
            pl.BlockSpec((None, d, tn), lambda i, j: (layer, 0, j + 2 * nt)),
        ],
        out_specs=[
            pl.BlockSpec((bm, tn), lambda i, j: (i, j)),
            pl.BlockSpec((bm, tn), lambda i, j: (i, j)),
        ],
        out_shape=[
            jax.ShapeDtypeStruct((m, c), _BF16),
            jax.ShapeDtypeStruct((m, c), _F32),
        ],
        scratch_shapes=[pltpu.VMEM((bm, d), _BF16)],
        compiler_params=_params(("arbitrary", "arbitrary")),
        name="inproj_odd",
    )(x, g, w, w, w)


def _outproj_kernel(a_ref, w_ref, x_ref, o_ref):
    o_ref[...] = x_ref[...] + _dot(a_ref[...], w_ref[...])


def _outproj_residual(a, w, layer, x):
    m, k = a.shape
    n = w.shape[2]
    bm = _ROW_BLOCK // 2
    return pl.pallas_call(
        _outproj_kernel,
        grid=(m // bm,),
        in_specs=[
            pl.BlockSpec((bm, k), lambda i: (i, 0)),
            pl.BlockSpec((None, k, n), lambda i: (layer, 0, 0), pipeline_mode=pl.Buffered(1)),
            pl.BlockSpec((bm, n), lambda i: (i, 0)),
        ],
        out_specs=pl.BlockSpec((bm, n), lambda i: (i, 0)),
        out_shape=jax.ShapeDtypeStruct((m, n), _F32),
        compiler_params=_params(("arbitrary",)),
        name="outproj_residual",
    )(a, w, x)


def _ffn_kernel(x_ref, g_ref, wup_ref, wdn_ref, o_ref, hn_ref):
    @pl.when(pl.program_id(1) == 0)
    def _():
        x = x_ref[...]
        hn_ref[...] = _rmsnorm_rows(x, g_ref[...]).astype(_BF16)
        o_ref[...] = x

    h = _dot(hn_ref[...], wup_ref[...].astype(_BF16))
    h = jnp.square(jnp.maximum(h, 0.0)).astype(_BF16)
    o_ref[...] += _dot(h, wdn_ref[...].astype(_BF16))


def _ffn_residual(x, g, w_up, w_down, layer):
    m, d = x.shape
    f = w_up.shape[2]
    bm, tf = _ROW_BLOCK, _COL_TILE
    return pl.pallas_call(
        _ffn_kernel,
        grid=(m // bm, f // tf),
        in_specs=[
            pl.BlockSpec((bm, d), lambda i, j: (i, 0), pipeline_mode=pl.Buffered(1)),
            pl.BlockSpec((1, d), lambda i, j: (0, 0)),
            pl.BlockSpec((None, d, tf), lambda i, j: (layer, 0, j)),
            pl.BlockSpec((None, tf, d), lambda i, j: (layer, j, 0)),
        ],
        out_specs=pl.BlockSpec((bm, d), lambda i, j: (i, 0)),
        out_shape=jax.ShapeDtypeStruct((m, d), _F32),
        scratch_shapes=[pltpu.VMEM((bm, d), _BF16)],
        compiler_params=_params(("arbitrary", "arbitrary")),
        name="ffn_residual",
    )(x, g, w_up, w_down)


def _final_norm_kernel(x_ref, g_ref, o_ref):
    o_ref[...] = _rmsnorm_rows(x_ref[...], g_ref[...])


def _final_norm(x, g, row_start, rows):
    d = x.shape[1]
    br = 512
    first = row_start // br
    return pl.pallas_call(
        _final_norm_kernel,
        grid=(rows // br,),
        in_specs=[
            pl.BlockSpec((br, d), lambda i: (i + first, 0)),
            pl.BlockSpec((1, d), lambda i: (0, 0)),
        ],
        out_specs=pl.BlockSpec((br, d), lambda i: (i, 0)),
        out_shape=jax.ShapeDtypeStruct((rows, d), _F32),
        compiler_params=_params(("arbitrary",)),
        name="final_norm",
    )(x, g)


def _split3(x):
    hi = x.astype(_BF16)
    r1 = x - hi.astype(_F32)
    mid = r1.astype(_BF16)
    lo = (r1 - mid.astype(_F32)).astype(_BF16)
    return hi, mid, lo


def _mixer_even_kernel(*refs, rows, a_chunk, a_seq, seqs_per_group, carry_state):
    (u_ref, v_ref, q_ref, k_ref, vb_ref, r_ref, la_ref, wmix_ref, bcol_ref, gout_ref) = refs[:10]
    rest = refs[10:]
    if carry_state:
        mix_ref, sout_ref = rest[-2:]
        sin_ref = None
    else:
        sin_ref = rest[0]
        mix_ref, sout_ref = rest[-2:]

    tt = lax.broadcasted_iota(jnp.int32, (a_chunk, a_chunk), 0)
    ss = lax.broadcasted_iota(jnp.int32, (a_chunk, a_chunk), 1)
    a_mask = (ss <= tt) & ((tt // a_seq) == (ss // a_seq))
    for h in range(_HEADS):
        cols = slice(h * _A_HEAD, (h + 1) * _A_HEAD)
        wm = jnp.where(a_mask, wmix_ref[h], 0.0).astype(_BF16)
        bias = bcol_ref[h]
        for c in range(rows // a_chunk):
            rws = slice(c * a_chunk, (c + 1) * a_chunk)
            mixed = _dot(wm, v_ref[rws, cols]) + bias
            mix_ref[rws, cols] = (u_ref[rws, cols].astype(_F32) * mixed).astype(_BF16)

    gr = _GLA_ROWS
    seq_rows = gr // seqs_per_group
    ti = lax.broadcasted_iota(jnp.int32, (gr, gr), 0)
    si = lax.broadcasted_iota(jnp.int32, (gr, gr), 1)
    same_seq = (ti // seq_rows) == (si // seq_rows)
    causal = (si <= ti) & same_seq
    cum_and_total = jnp.concatenate(
        [jnp.where(causal, 1.0, 0.0), jnp.where(same_seq, 1.0, 0.0)], axis=0).astype(_BF16)
    row_k = lax.broadcasted_iota(jnp.int32, (gr, _DK), 0)
    row_v = lax.broadcasted_iota(jnp.int32, (gr, _DV), 0)
    pos_k = row_k % seq_rows
    ones_v = jnp.ones((gr, _DV), _BF16)
    scale = _DK ** -0.5
    a_width = _HEADS * _A_HEAD

    if carry_state:
        @pl.when(pl.program_id(1) == 0)
        def _():
            sout_ref[...] = jnp.zeros(sout_ref.shape, _F32)

    for h in range(_HEADS):
        kc = slice(h * _DK, (h + 1) * _DK)
        vc = slice(h * _DV, (h + 1) * _DV)
        g_head = gout_ref[:, vc]
        state = sout_ref[h] if carry_state else None
        for gi in range(rows // gr):
            rws = slice(gi * gr, (gi + 1) * gr)
            la = la_ref[rws, kc]
            la_hi = la.astype(_BF16)
            la_lo = (la - la_hi.astype(_F32)).astype(_BF16)
            sums = _dot(cum_and_total, jnp.concatenate([la_hi, la_lo], axis=1))
            g_cum = sums[:gr, :_DK] + sums[:gr, _DK:]
            g_tot = sums[gr:, :_DK] + sums[gr:, _DK:]
            q = q_ref[rws, kc].astype(_F32) * scale
            k = k_ref[rws, kc].astype(_F32)
            v = vb_ref[rws, vc]
            q_in = (q * jnp.exp(g_cum)).astype(_BF16)
            k_in = (k * jnp.exp(-g_cum)).astype(_BF16)
            k_dec = (k * jnp.exp(g_tot - g_cum)).astype(_BF16)
            scores = jnp.where(causal, _dot_nt(q_in, k_in), 0.0).astype(_BF16)
            o = _dot(scores, v)
            t_hi, t_mid, t_lo = _split3(g_tot)
            zero = jnp.zeros_like(t_hi)
            pieces = jnp.where(pos_k == 0, t_hi,
                               jnp.where(pos_k == 1, t_mid,
                                         jnp.where(pos_k == 2, t_lo, zero)))
            if carry_state:
                o = o + _dot(q_in, state.astype(_BF16))
                decay = jnp.exp(_dot_tn(pieces, ones_v))
                state = state * decay + _dot_tn(k_dec, v)
            else:
                for s in range(seqs_per_group):
                    in_seq_k = (row_k // seq_rows) == s
                    in_seq_v = (row_v // seq_rows) == s
                    s0 = sin_ref[s, h]
                    o = o + jnp.where(in_seq_v, _dot(q_in, s0.astype(_BF16)), 0.0)
                    decay = jnp.exp(_dot_tn(jnp.where(in_seq_k, pieces, zero), ones_v))
                    kd = jnp.where(in_seq_k, k_dec, jnp.zeros_like(k_dec))
                    sout_ref[s, h] = s0 * decay + _dot_tn(kd, v)
            ms = jnp.mean(o * o, axis=-1, keepdims=True)
            on = (o * lax.rsqrt(ms + _EPS)) * g_head
            r = r_ref[rws, vc].astype(_F32)
            gated = on * (r * (1.0 / (1.0 + jnp.exp(-r))))
            mix_ref[rws, a_width + h * _DV:a_width + (h + 1) * _DV] = gated.astype(_BF16)
        if carry_state:
            sout_ref[h] = state


def _mixer_even(proj, loga, wmix, bcol, gout, *, layer_slot, n_slots, total_rows,
                row_start, n_seqs, seq_len, state_in=None, mix_prev=None, state_prev=None):
    carry = state_in is None
    a_width = _HEADS * _A_HEAD
    kw = _HEADS * _DK
    if carry:
        rows = 256
        blocks_per_seq = seq_len // rows
        grid = (n_seqs, blocks_per_seq)
        first = row_start // rows
        rmap = lambda col: (lambda n, c: (first + n * blocks_per_seq + c, col))
        a_chunk, a_seq, spg = _A_CHUNK, _A_CHUNK, 1
        state_shape = (n_slots, n_seqs, _HEADS, _DK, _DV)
        state_spec = pl.BlockSpec((None, None, _HEADS, _DK, _DV),
                                  lambda n, c: (layer_slot, n, 0, 0, 0))
        sem = ("arbitrary", "arbitrary")
    else:
        rows = _GLA_ROWS
        spg = rows // seq_len
        grid = (n_seqs // spg,)
        first = row_start // rows
        rmap = lambda col: (lambda g: (first + g, col))
        a_chunk, a_seq = rows, seq_len
        state_shape = (n_slots, n_seqs, _HEADS, _DK, _DV)
        state_spec = pl.BlockSpec((None, spg, _HEADS, _DK, _DV),
                                  lambda g: (layer_slot, g, 0, 0, 0))
        sem = ("arbitrary",)
    const = (lambda *_: (0, 0, 0))
    in_specs = [
        pl.BlockSpec((rows, a_width), rmap(0)),
        pl.BlockSpec((rows, a_width), rmap(1)),
        pl.BlockSpec((rows, kw), rmap(2 * a_width // kw)),
        pl.BlockSpec((rows, kw), rmap(2 * a_width // kw + 1)),
        pl.BlockSpec((rows, a_width), rmap(3)),
        pl.BlockSpec((rows, a_width), rmap(4)),
        pl.BlockSpec((rows, kw), rmap(0)),
        pl.BlockSpec(wmix.shape, const),
        pl.BlockSpec(bcol.shape, const),
        pl.BlockSpec(gout.shape, lambda *_: (0, 0)),
    ]
    args = [proj, proj, proj, proj, proj, proj, loga, wmix, bcol, gout]
    if not carry:
        in_specs.append(state_spec)
        args.append(state_in)
    aliases = {}
    if mix_prev is not None:
        aliases[len(args)] = 0
        in_specs.append(pl.BlockSpec(memory_space=pl.ANY))
        args.append(mix_prev)
    if state_prev is not None:
        aliases[len(args)] = 1
        in_specs.append(pl.BlockSpec(memory_space=pl.ANY))
        args.append(state_prev)
    return pl.pallas_call(
        functools.partial(_mixer_even_kernel, rows=rows, a_chunk=a_chunk, a_seq=a_seq,
                          seqs_per_group=spg, carry_state=carry),
        grid=grid,
        in_specs=in_specs,
        out_specs=[pl.BlockSpec((rows, 2 * a_width), rmap(0)), state_spec],
        out_shape=[jax.ShapeDtypeStruct((total_rows, 2 * a_width), _BF16),
                   jax.ShapeDtypeStruct(state_shape, _F32)],
        input_output_aliases=aliases,
        compiler_params=_params(sem),
        name="mixer_even_long" if carry else "mixer_even_short",
    )(*args)


def _conv_gate_kernel(*refs, rows, seq_len, has_halo, has_hist):
    refs = list(refs)
    z_ref = refs.pop(0)
    halo_ref = refs.pop(0) if has_halo else None
    bg_ref = refs.pop(0)
    cw_ref = refs.pop(0)
    h1_ref = refs.pop(0) if has_hist else None
    h2_ref = refs.pop(0) if has_hist else None
    out_ref, zs_ref = refs[-2:]
    cols = z_ref.shape[1]

    z = z_ref[...]
    zs_ref[8:8 + rows, :] = z
    zs_ref[0:8, :] = halo_ref[...] if has_halo else jnp.zeros((8, cols), _F32)
    prev1 = zs_ref[pl.ds(7, rows), :]
    prev2 = zs_ref[pl.ds(6, rows), :]
    t = (pl.program_id(0) * rows + lax.broadcasted_iota(jnp.int32, (rows, cols), 0)) % seq_len
    hist1 = h1_ref[...] if has_hist else 0.0
    hist2 = h2_ref[...] if has_hist else 0.0
    prev1 = jnp.where(t >= 1, prev1, hist1)
    prev2 = jnp.where(t >= 2, prev2, hist2)
    conv = prev2 * cw_ref[0:1, :] + prev1 * cw_ref[1:2, :] + z * cw_ref[2:3, :]
    out_ref[...] = (bg_ref[...].astype(_F32) * conv).astype(_BF16)


def _conv_gate(z, bg, cw, *, row_start, n_rows, seq_len, hist=None, prev=None):
    total_rows, c = z.shape
    rows, tn = 512, 1024
    first = row_start // rows
    has_halo = seq_len > rows
    has_hist = hist is not None
    main = lambda i, j: (first + i, j)
    in_specs = [pl.BlockSpec((rows, tn), main)]
    args = [z]
    if has_halo:
        in_specs.append(pl.BlockSpec(
            (8, tn), lambda i, j: (jnp.maximum((first + i) * (rows // 8) - 1, 0), j)))
        args.append(z)
    in_specs += [pl.BlockSpec((rows, tn), main), pl.BlockSpec((_CONV_TAPS, tn), lambda i, j: (0, j))]
    args += [bg, cw]
    if has_hist:
        in_specs += [pl.BlockSpec((rows, tn), lambda i, j: (i, j))] * 2
        args += list(hist)
    aliases = {}
    if prev is not None:
        aliases[len(args)] = 0
        in_specs.append(pl.BlockSpec(memory_space=pl.ANY))
        args.append(prev)
    return pl.pallas_call(
        functools.partial(_conv_gate_kernel, rows=rows, seq_len=seq_len,
                          has_halo=has_halo, has_hist=has_hist),
        grid=(n_rows // rows, c // tn),
        in_specs=in_specs,
        out_specs=pl.BlockSpec((rows, tn), main),
        out_shape=jax.ShapeDtypeStruct((total_rows, c), _BF16),
        scratch_shapes=[pltpu.VMEM((rows + 8, tn), _F32)],
        input_output_aliases=aliases,
        compiler_params=_params(("arbitrary", "arbitrary")),
        name="conv_gate_long" if has_halo else "conv_gate_short",
    )(*args)


def kernel(x_prompt, x_sample, state_gla, state_conv, norm_mix, norm_ffn, norm_final, w_in_even, w_gate_up, b_gate, w_spatial, b_spatial, g_gla_out, w_out_even, w_in_odd, conv_w, w_out_odd, w_ffn_up, w_ffn_down):
    n_p, t_p, d = x_prompt.shape
    n_s, t_s, _ = x_sample.shape
    rows_p, rows_s = n_p * t_p, n_s * t_s
    total = rows_p + rows_s
    depth = norm_mix.shape[0]
    n_even = w_in_even.shape[0]
    a_width = _HEADS * _A_HEAD
    main_cols = w_in_even.shape[2] - _GATE_RANK

    x = jnp.concatenate([x_prompt.reshape(rows_p, d), x_sample.reshape(rows_s, d)], axis=0)

    reps = _GLA_ROWS // t_s
    wmix_s = jnp.tile(w_spatial[:, :, :t_s, :t_s], (1, 1, reps, reps))
    bcol_s = jnp.tile(b_spatial[:, :, :t_s], (1, 1, reps))[..., None]
    bcol_p = b_spatial[..., None]

    w_out_even_b = w_out_even.astype(_BF16)
    w_out_odd_b = w_out_odd.astype(_BF16)

    v_rows, conv_p, conv_s = [], [], []
    gla_p = gla_s = None
    for l in range(depth):
        i = l // 2
        g_mix = norm_mix[l][None, :]
        if l % 2 == 0:
            w_glr = jnp.pad(w_in_even[i, :, main_cols:], ((0, 0), (0, 128 - _GATE_RANK))).astype(_BF16)
            w_gate = jnp.pad(w_gate_up[i], ((0, 128 - _GATE_RANK), (0, 0))).astype(_BF16)
            proj, v32, loga = _inproj_even(x, g_mix, w_in_even, i, w_glr, w_gate, b_gate[i][None, :],
                                           tail_rows=rows_s)
            gout = g_gla_out[i][None, :]
            mix, gla_p = _mixer_even(
                proj, loga, w_spatial[i], bcol_p[i], gout, layer_slot=i, n_slots=n_even,
                total_rows=total, row_start=0, n_seqs=n_p, seq_len=t_p, state_prev=gla_p)
            mix, gla_s = _mixer_even(
                proj, loga, wmix_s[i], bcol_s[i], gout, layer_slot=i, n_slots=n_even,
                total_rows=total, row_start=rows_p, n_seqs=n_s, seq_len=t_s,
                state_in=state_gla, mix_prev=mix, state_prev=gla_s)
            v_rows.append(v32.reshape(n_s, t_s, a_width))
            x = _outproj_residual(mix, w_out_even_b, i, x)
        else:
            bg, z = _inproj_odd(x, g_mix, w_in_odd, i)
            buf = state_conv[i].astype(_F32)
            keep = _CONV_TAPS - 1
            hist1 = jnp.pad(buf[:, 1:2], ((0, 0), (0, t_s - 1), (0, 0))).reshape(rows_s, d)
            hist2 = jnp.pad(buf, ((0, 0), (0, t_s - keep), (0, 0))).reshape(rows_s, d)
            gated = _conv_gate(z, bg, conv_w[i], row_start=0, n_rows=rows_p, seq_len=t_p)
            gated = _conv_gate(z, bg, conv_w[i], row_start=rows_p, n_rows=rows_s, seq_len=t_s,
                               hist=(hist1, hist2), prev=gated)
            conv_p.append(jnp.stack([z[(n + 1) * t_p - keep:(n + 1) * t_p] for n in range(n_p)]))
            conv_s.append(z[rows_p:].reshape(n_s, t_s, d)[:, t_s - keep:])
            x = _outproj_residual(gated, w_out_odd_b, i, x)
        x = _ffn_residual(x, norm_ffn[l][None, :], w_ffn_up, w_ffn_down, l)

    g_fin = norm_final[None, :]
    y_prompt = _final_norm(x, g_fin, 0, rows_p).reshape(n_p, t_p, d)
    y_sample = _final_norm(x, g_fin, rows_p, rows_s).reshape(n_s, t_s, d)
    return (y_prompt, y_sample, gla_p, gla_s, jnp.stack(conv_p), jnp.stack(conv_s), jnp.stack(v_rows))
```

```python
import functools

import jax
import jax.numpy as jnp
from jax import lax
from jax.experimental import pallas as pl
from jax.experimental.pallas import tpu as pltpu

_F32 = jnp.float32
_BF16 = jnp.bfloat16

_EPS = 1e-6
_HEADS = 4
_A_HEAD = 256
_A_CHUNK = 128
_DK = 128
_DV = 256
_GATE_RANK = 16
_GATE_TAU = 16.0
_GLA_ROWS = 64
_CONV_TAPS = 3

_ROW_BLOCK = 1088
_COL_TILE = 512
_MIB = 1024 * 1024
_VMEM_LIMIT = 56 * _MIB
_VMEM_LIMIT_FFN = 62 * _MIB


def _params(semantics, vmem_limit=_VMEM_LIMIT):
    return pltpu.CompilerParams(dimension_semantics=semantics, vmem_limit_bytes=vmem_limit)


def _rmsnorm_rows(x, g):
    ms = jnp.mean(x * x, axis=-1, keepdims=True)
    return (x * lax.rsqrt(ms + _EPS)) * g


def _gelu_tanh(x):
    return x * (0.5 * (1.0 + jnp.tanh(0.7978845608028654 * (x + 0.044715 * (x * x * x)))))


def _log_sigmoid(x):
    return jnp.minimum(x, 0.0) - jnp.log(1.0 + jnp.exp(-jnp.abs(x)))


def _dot(a, b):
    return jnp.dot(a, b, preferred_element_type=_F32)


def _dot_nt(a, b):
    return lax.dot_general(a, b, (((1,), (1,)), ((), ())), preferred_element_type=_F32)


def _dot_tn(a, b):
    return lax.dot_general(a, b, (((0,), (0,)), ((), ())), preferred_element_type=_F32)


def _inproj_even_kernel(x_ref, g_ref, w_ref, wglr_ref, wgate_ref, bgate_ref,
                        proj_ref, v32_ref, loga_ref, hn_ref, *, gelu_tiles, v_tiles, tail_start):
    i, j = pl.program_id(0), pl.program_id(1)

    @pl.when(j == 0)
    def _():
        hn = _rmsnorm_rows(x_ref[...], g_ref[...]).astype(_BF16)
        hn_ref[...] = hn
        glr = _dot(hn, wglr_ref[...])
        gate = _dot(glr.astype(_BF16), wgate_ref[...]) + bgate_ref[...]
        loga_ref[...] = _log_sigmoid(gate) * (1.0 / _GATE_TAU)

    acc = _dot(hn_ref[...], w_ref[...])
    proj_ref[...] = jnp.where(j < gelu_tiles, _gelu_tanh(acc), acc).astype(_BF16)

    @pl.when((i == pl.num_programs(0) - 1) & (j >= v_tiles[0]) & (j < v_tiles[1]))
    def _():
        v32_ref[...] = _gelu_tanh(_dot(hn_ref[tail_start:, :], w_ref[...]))


def _inproj_even(x, g, w_all, layer, w_glr, w_gate, b_gate, tail_rows):
    m, d = x.shape
    n = w_all.shape[2] - _GATE_RANK
    bm, tn = _ROW_BLOCK, _COL_TILE
    assert tail_rows <= bm and (bm - tail_rows) % 16 == 0
    a_width = _HEADS * _A_HEAD
    gelu_tiles = 2 * a_width // tn
    v_tiles = (a_width // tn, 2 * a_width // tn)
    nk = w_gate.shape[1]

    def v32_map(i, j):
        tile = jnp.clip(j - v_tiles[0], 0, v_tiles[1] - v_tiles[0] - 1)
        return (0, jnp.where(i == m // bm - 1, tile, 0))

    return pl.pallas_call(
        functools.partial(_inproj_even_kernel, gelu_tiles=gelu_tiles, v_tiles=v_tiles,
                          tail_start=bm - tail_rows),
        grid=(m // bm, n // tn),
        in_specs=[
            pl.BlockSpec((bm, d), lambda i, j: (i, 0)),
            pl.BlockSpec((1, d), lambda i, j: (0, 0)),
            pl.BlockSpec((None, d, tn), lambda i, j: (layer, 0, j)),
            pl.BlockSpec(w_glr.shape, lambda i, j: (0, 0)),
            pl.BlockSpec(w_gate.shape, lambda i, j: (0, 0)),
            pl.BlockSpec((1, nk), lambda i, j: (0, 0)),
        ],
        out_specs=[
            pl.BlockSpec((bm, tn), lambda i, j: (i, j)),
            pl.BlockSpec((tail_rows, tn), v32_map),
            pl.BlockSpec((bm, nk), lambda i, j: (i, 0)),
        ],
        out_shape=[
            jax.ShapeDtypeStruct((m, n), _BF16),
            jax.ShapeDtypeStruct((tail_rows, a_width), _F32),
            jax.ShapeDtypeStruct((m, nk), _F32),
        ],
        scratch_shapes=[pltpu.VMEM((bm, d), _BF16)],
        compiler_params=_params(("arbitrary", "arbitrary")),
        name="inproj_even",
    )(x, g, w_all, w_glr, w_gate, b_gate)


def _inproj_odd_kernel(x_ref, g_ref, wb_ref, wc_ref, wh_ref, bg_ref, z_ref, hn_ref):
    @pl.when(pl.program_id(1) == 0)
    def _():
        hn_ref[...] = _rmsnorm_rows(x_ref[...], g_ref[...]).astype(_BF16)

    hn = hn_ref[...]
    bg_ref[...] = _dot(hn, wb_ref[...]).astype(_BF16)
    z_ref[...] = _dot(hn, wc_ref[...]) * _dot(hn, wh_ref[...])


def _inproj_odd(x, g, w, layer):
    m, d = x.shape
    c = w.shape[2] // 3
    bm, tn = _ROW_BLOCK, _COL_TILE
    nt = c // tn
    return pl.pallas_call(
        _inproj_odd_kernel,
        grid=(m // bm, nt),
        in_specs=[
            pl.BlockSpec((bm, d), lambda i, j: (i, 0)),
            pl.BlockSpec((1, d), lambda i, j: (0, 0)),
            pl.BlockSpec((None, d, tn), lambda i, j: (layer, 0, j)),
            pl.BlockSpec((None, d, tn), lambda i, j: (layer, 0, j + nt)),
            pl.BlockSpec((None, d, tn), lambda i, j: (layer, 0, j + 2 * nt)),
        ],
        out_specs=[
            pl.BlockSpec((bm, tn), lambda i, j: (i, j)),
            pl.BlockSpec((bm, tn), lambda i, j: (i, j)),
        ],
        out_shape=[
            jax.ShapeDtypeStruct((m, c), _BF16),
            jax.ShapeDtypeStruct((m, c), _F32),
        ],
        scratch_shapes=[pltpu.VMEM((bm, d), _BF16)],
        compiler_params=_params(("arbitrary", "arbitrary")),
        name="inproj_odd",
    )(x, g, w, w, w)


def _outproj_kernel(a_ref, w_ref, x_ref, o_ref):
    o_ref[...] = x_ref[...] + _dot(a_ref[...], w_ref[...])


def _outproj_residual(a, w, layer, x):
    m, k = a.shape
    n = w.shape[2]
    bm = _ROW_BLOCK // 2
    return pl.pallas_call(
        _outproj_kernel,
        grid=(m // bm,),
        in_specs=[
            pl.BlockSpec((bm, k), lambda i: (i, 0)),
            pl.BlockSpec((None, k, n), lambda i: (layer, 0, 0), pipeline_mode=pl.Buffered(1)),
            pl.BlockSpec((bm, n), lambda i: (i, 0)),
        ],
        out_specs=pl.BlockSpec((bm, n), lambda i: (i, 0)),
        out_shape=jax.ShapeDtypeStruct((m, n), _F32),
        compiler_params=_params(("arbitrary",)),
        name="outproj_residual",
    )(a, w, x)


def _ffn_kernel(x_ref, g_ref, wup_ref, wdn_ref, *rest, tail_start):
    final = len(rest) == 4
    if final:
        gfin_ref, o_ref, tail_ref, hn_ref = rest
    else:
        o_ref, hn_ref = rest
    i, j = pl.program_id(0), pl.program_id(1)

    @pl.when(j == 0)
    def _():
        x = x_ref[...]
        hn_ref[...] = _rmsnorm_rows(x, g_ref[...]).astype(_BF16)
        o_ref[...] = x

    h = _dot(hn_ref[...], wup_ref[...].astype(_BF16))
    h = jnp.square(jnp.maximum(h, 0.0)).astype(_BF16)
    o_ref[...] += _dot(h, wdn_ref[...].astype(_BF16))

    if final:
        @pl.when(j == pl.num_programs(1) - 1)
        def _():
            o_ref[...] = _rmsnorm_rows(o_ref[...], gfin_ref[...])

            @pl.when(i == pl.num_programs(0) - 1)
            def _():
                tail_ref[...] = o_ref[tail_start:, :]


def _ffn_residual(x, g, w_up, w_down, layer, final_gain=None, tail_rows=0):
    m, d = x.shape
    f = w_up.shape[2]
    bm, tf = _ROW_BLOCK, _COL_TILE
    in_specs = [
        pl.BlockSpec((bm, d), lambda i, j: (i, 0)),
        pl.BlockSpec((1, d), lambda i, j: (0, 0)),
        pl.BlockSpec((None, d, tf), lambda i, j: (layer, 0, j)),
        pl.BlockSpec((None, tf, d), lambda i, j: (layer, j, 0)),
    ]
    args = [x, g, w_up, w_down]
    row_block = pl.BlockSpec((bm, d), lambda i, j: (i, 0))
    if final_gain is None:
        out_specs = row_block
        out_shape = jax.ShapeDtypeStruct((m, d), _F32)
    else:
        assert tail_rows <= bm and (bm - tail_rows) % 8 == 0
        in_specs.append(pl.BlockSpec((1, d), lambda i, j: (0, 0)))
        args.append(final_gain)
        out_specs = [row_block, pl.BlockSpec((tail_rows, d), lambda i, j: (0, 0))]
        out_shape = [jax.ShapeDtypeStruct((m - tail_rows, d), _F32),
                     jax.ShapeDtypeStruct((tail_rows, d), _F32)]
    return pl.pallas_call(
        functools.partial(_ffn_kernel, tail_start=bm - tail_rows),
        grid=(m // bm, f // tf),
        in_specs=in_specs,
        out_specs=out_specs,
        out_shape=out_shape,
        scratch_shapes=[pltpu.VMEM((bm, d), _BF16)],
        compiler_params=_params(("arbitrary", "arbitrary"), _VMEM_LIMIT_FFN),
        name="ffn_residual" if final_gain is None else "ffn_final",
    )(*args)


def _split3(x):
    hi = x.astype(_BF16)
    r1 = x - hi.astype(_F32)
    mid = r1.astype(_BF16)
    lo = (r1 - mid.astype(_F32)).astype(_BF16)
    return hi, mid, lo


def _mixer_even_kernel(*refs, rows, a_chunk, a_seq, seqs_per_group, carry_state):
    (u_ref, v_ref, q_ref, k_ref, vb_ref, r_ref, la_ref, wmix_ref, bcol_ref, gout_ref) = refs[:10]
    rest = refs[10:]
    if carry_state:
        mix_ref, sout_ref = rest[-2:]
        sin_ref = None
    else:
        sin_ref = rest[0]
        mix_ref, sout_ref = rest[-2:]

    tt = lax.broadcasted_iota(jnp.int32, (a_chunk, a_chunk), 0)
    ss = lax.broadcasted_iota(jnp.int32, (a_chunk, a_chunk), 1)
    a_mask = (ss <= tt) & ((tt // a_seq) == (ss // a_seq))
    for h in range(_HEADS):
        cols = slice(h * _A_HEAD, (h + 1) * _A_HEAD)
        wm = jnp.where(a_mask, wmix_ref[h], 0.0).astype(_BF16)
        bias = bcol_ref[h]
        for c in range(rows // a_chunk):
            rws = slice(c * a_chunk, (c + 1) * a_chunk)
            mixed = _dot(wm, v_ref[rws, cols]) + bias
            mix_ref[rws, cols] = (u_ref[rws, cols].astype(_F32) * mixed).astype(_BF16)

    gr = _GLA_ROWS
    seq_rows = gr // seqs_per_group
    ti = lax.broadcasted_iota(jnp.int32, (gr, gr), 0)
    si = lax.broadcasted_iota(jnp.int32, (gr, gr), 1)
    same_seq = (ti // seq_rows) == (si // seq_rows)
    causal = (si <= ti) & same_seq
    cum_and_total = jnp.concatenate(
        [jnp.where(causal, 1.0, 0.0), jnp.where(same_seq, 1.0, 0.0)], axis=0).astype(_BF16)
    kw = _HEADS * _DK
    row_k = lax.broadcasted_iota(jnp.int32, (gr, _DK), 0)
    row_v = lax.broadcasted_iota(jnp.int32, (gr, _DV), 0)
    pos_all = lax.broadcasted_iota(jnp.int32, (gr, kw), 0) % seq_rows
    ones_v = jnp.ones((gr, _DV), _BF16)
    scale = _DK ** -0.5
    a_width = _HEADS * _A_HEAD

    if carry_state:
        @pl.when(pl.program_id(1) == 0)
        def _():
            sout_ref[...] = jnp.zeros(sout_ref.shape, _F32)
        states = [sout_ref[h] for h in range(_HEADS)]

    for gi in range(rows // gr):
        rws = slice(gi * gr, (gi + 1) * gr)
        la = la_ref[rws, :]
        la_hi = la.astype(_BF16)
        la_lo = (la - la_hi.astype(_F32)).astype(_BF16)
        sums = _dot(cum_and_total, jnp.concatenate([la_hi, la_lo], axis=1))
        g_cum = sums[:gr, :kw] + sums[:gr, kw:]
        g_tot = sums[gr:, :kw] + sums[gr:, kw:]
        q = q_ref[rws, :].astype(_F32) * scale
        k = k_ref[rws, :].astype(_F32)
        q_in_all = (q * jnp.exp(g_cum)).astype(_BF16)
        k_in_all = (k * jnp.exp(-g_cum)).astype(_BF16)
        k_dec_all = (k * jnp.exp(g_tot - g_cum)).astype(_BF16)
        t_hi, t_mid, t_lo = _split3(g_tot)
        pieces_all = jnp.where(pos_all == 0, t_hi,
                               jnp.where(pos_all == 1, t_mid,
                                         jnp.where(pos_all == 2, t_lo, jnp.zeros_like(t_hi))))
        for h in range(_HEADS):
            kc = slice(h * _DK, (h + 1) * _DK)
            vc = slice(h * _DV, (h + 1) * _DV)
            q_in, k_in, k_dec, pieces = q_in_all[:, kc], k_in_all[:, kc], k_dec_all[:, kc], pieces_all[:, kc]
            v = vb_ref[rws, vc]
            scores = jnp.where(causal, _dot_nt(q_in, k_in), 0.0).astype(_BF16)
            o = _dot(scores, v)
            if carry_state:
                o = o + _dot(q_in, states[h].astype(_BF16))
                decay = jnp.exp(_dot_tn(pieces, ones_v))
                states[h] = states[h] * decay + _dot_tn(k_dec, v)
            else:
                zero = jnp.zeros_like(pieces)
                for s in range(seqs_per_group):
                    in_seq_k = (row_k // seq_rows) == s
                    in_seq_v = (row_v // seq_rows) == s
                    s0 = sin_ref[s, h]
                    o = o + jnp.where(in_seq_v, _dot(q_in, s0.astype(_BF16)), 0.0)
                    decay = jnp.exp(_dot_tn(jnp.where(in_seq_k, pieces, zero), ones_v))
                    kd = jnp.where(in_seq_k, k_dec, zero)
                    sout_ref[s, h] = s0 * decay + _dot_tn(kd, v)
            ms = jnp.mean(o * o, axis=-1, keepdims=True)
            on = (o * lax.rsqrt(ms + _EPS)) * gout_ref[:, vc]
            r = r_ref[rws, vc].astype(_F32)
            gated = on * (r * (1.0 / (1.0 + jnp.exp(-r))))
            mix_ref[rws, a_width + h * _DV:a_width + (h + 1) * _DV] = gated.astype(_BF16)

    if carry_state:
        for h in range(_HEADS):
            sout_ref[h] = states[h]


def _mixer_even(proj, loga, wmix, bcol, gout, *, layer_slot, n_slots, total_rows,
                row_start, n_seqs, seq_len, state_in=None, mix_prev=None, state_prev=None):
    carry = state_in is None
    a_width = _HEADS * _A_HEAD
    kw = _HEADS * _DK
    if carry:
        rows = 256
        blocks_per_seq = seq_len // rows
        grid = (n_seqs, blocks_per_seq)
        first = row_start // rows
        rmap = lambda col: (lambda n, c: (first + n * blocks_per_seq + c, col))
        a_chunk, a_seq, spg = _A_CHUNK, _A_CHUNK, 1
        state_shape = (n_slots, n_seqs, _HEADS, _DK, _DV)
        state_spec = pl.BlockSpec((None, None, _HEADS, _DK, _DV),
                                  lambda n, c: (layer_slot, n, 0, 0, 0))
        sem = ("arbitrary", "arbitrary")
    else:
        rows = _GLA_ROWS
        spg = rows // seq_len
        grid = (n_seqs // spg,)
        first = row_start // rows
        rmap = lambda col: (lambda g: (first + g, col))
        a_chunk, a_seq = rows, seq_len
        state_shape = (n_slots, n_seqs, _HEADS, _DK, _DV)
        state_spec = pl.BlockSpec((None, spg, _HEADS, _DK, _DV),
                                  lambda g: (layer_slot, g, 0, 0, 0))
        sem = ("arbitrary",)
    const = (lambda *_: (0, 0, 0))
    in_specs = [
        pl.BlockSpec((rows, a_width), rmap(0)),
        pl.BlockSpec((rows, a_width), rmap(1)),
        pl.BlockSpec((rows, kw), rmap(2 * a_width // kw)),
        pl.BlockSpec((rows, kw), rmap(2 * a_width // kw + 1)),
        pl.BlockSpec((rows, a_width), rmap(3)),
        pl.BlockSpec((rows, a_width), rmap(4)),
        pl.BlockSpec((rows, kw), rmap(0)),
        pl.BlockSpec(wmix.shape, const),
        pl.BlockSpec(bcol.shape, const),
        pl.BlockSpec(gout.shape, lambda *_: (0, 0)),
    ]
    args = [proj, proj, proj, proj, proj, proj, loga, wmix, bcol, gout]
    if not carry:
        in_specs.append(state_spec)
        args.append(state_in)
    aliases = {}
    if mix_prev is not None:
        aliases[len(args)] = 0
        in_specs.append(pl.BlockSpec(memory_space=pl.ANY))
        args.append(mix_prev)
    if state_prev is not None:
        aliases[len(args)] = 1
        in_specs.append(pl.BlockSpec(memory_space=pl.ANY))
        args.append(state_prev)
    return pl.pallas_call(
        functools.partial(_mixer_even_kernel, rows=rows, a_chunk=a_chunk, a_seq=a_seq,
                          seqs_per_group=spg, carry_state=carry),
        grid=grid,
        in_specs=in_specs,
        out_specs=[pl.BlockSpec((rows, 2 * a_width), rmap(0)), state_spec],
        out_shape=[jax.ShapeDtypeStruct((total_rows, 2 * a_width), _BF16),
                   jax.ShapeDtypeStruct(state_shape, _F32)],
        input_output_aliases=aliases,
        compiler_params=_params(sem),
        name="mixer_even_long" if carry else "mixer_even_short",
    )(*args)


def _conv_gate_kernel(*refs, rows, seq_len, has_halo, has_hist):
    refs = list(refs)
    z_ref = refs.pop(0)
    halo_ref = refs.pop(0) if has_halo else None
    bg_ref = refs.pop(0)
    cw_ref = refs.pop(0)
    h1_ref = refs.pop(0) if has_hist else None
    h2_ref = refs.pop(0) if has_hist else None
    out_ref, zs_ref = refs[-2:]
    cols = z_ref.shape[1]

    z = z_ref[...]
    zs_ref[8:8 + rows, :] = z
    zs_ref[0:8, :] = halo_ref[...] if has_halo else jnp.zeros((8, cols), _F32)
    prev1 = zs_ref[pl.ds(7, rows), :]
    prev2 = zs_ref[pl.ds(6, rows), :]
    t = (pl.program_id(0) * rows + lax.broadcasted_iota(jnp.int32, (rows, cols), 0)) % seq_len
    hist1 = h1_ref[...] if has_hist else 0.0
    hist2 = h2_ref[...] if has_hist else 0.0
    prev1 = jnp.where(t >= 1, prev1, hist1)
    prev2 = jnp.where(t >= 2, prev2, hist2)
    conv = prev2 * cw_ref[0:1, :] + prev1 * cw_ref[1:2, :] + z * cw_ref[2:3, :]
    out_ref[...] = (bg_ref[...].astype(_F32) * conv).astype(_BF16)


def _conv_gate(z, bg, cw, *, row_start, n_rows, seq_len, hist=None, prev=None):
    total_rows, c = z.shape
    rows, tn = 512, 1024
    first = row_start // rows
    has_halo = seq_len > rows
    has_hist = hist is not None
    main = lambda i, j: (first + i, j)
    in_specs = [pl.BlockSpec((rows, tn), main)]
    args = [z]
    if has_halo:
        in_specs.append(pl.BlockSpec(
            (8, tn), lambda i, j: (jnp.maximum((first + i) * (rows // 8) - 1, 0), j)))
        args.append(z)
    in_specs += [pl.BlockSpec((rows, tn), main), pl.BlockSpec((_CONV_TAPS, tn), lambda i, j: (0, j))]
    args += [bg, cw]
    if has_hist:
        in_specs += [pl.BlockSpec((rows, tn), lambda i, j: (i, j))] * 2
        args += list(hist)
    aliases = {}
    if prev is not None:
        aliases[len(args)] = 0
        in_specs.append(pl.BlockSpec(memory_space=pl.ANY))
        args.append(prev)
    return pl.pallas_call(
        functools.partial(_conv_gate_kernel, rows=rows, seq_len=seq_len,
                          has_halo=has_halo, has_hist=has_hist),
        grid=(n_rows // rows, c // tn),
        in_specs=in_specs,
        out_specs=pl.BlockSpec((rows, tn), main),
        out_shape=jax.ShapeDtypeStruct((total_rows, c), _BF16),
        scratch_shapes=[pltpu.VMEM((rows + 8, tn), _F32)],
        input_output_aliases=aliases,
        compiler_params=_params(("arbitrary", "arbitrary")),
        name="conv_gate_long" if has_halo else "conv_gate_short",
    )(*args)


def kernel(x_prompt, x_sample, state_gla, state_conv, norm_mix, norm_ffn, norm_final, w_in_even, w_gate_up, b_gate, w_spatial, b_spatial, g_gla_out, w_out_even, w_in_odd, conv_w, w_out_odd, w_ffn_up, w_ffn_down):
    n_p, t_p, d = x_prompt.shape
    n_s, t_s, _ = x_sample.shape
    rows_p, rows_s = n_p * t_p, n_s * t_s
    total = rows_p + rows_s
    depth = norm_mix.shape[0]
    n_even = w_in_even.shape[0]
    a_width = _HEADS * _A_HEAD
    main_cols = w_in_even.shape[2] - _GATE_RANK

    x = jnp.concatenate([x_prompt.reshape(rows_p, d), x_sample.reshape(rows_s, d)], axis=0)

    reps = _GLA_ROWS // t_s
    wmix_s = jnp.tile(w_spatial[:, :, :t_s, :t_s], (1, 1, reps, reps))
    bcol_s = jnp.tile(b_spatial[:, :, :t_s], (1, 1, reps))[..., None]
    bcol_p = b_spatial[..., None]

    w_in_even_b = w_in_even.astype(_BF16)
    w_in_odd_b = w_in_odd.astype(_BF16)
    w_out_even_b = w_out_even.astype(_BF16)
    w_out_odd_b = w_out_odd.astype(_BF16)

    v_rows, conv_p, conv_s = [], [], []
    gla_p = gla_s = None
    for l in range(depth):
        i = l // 2
        g_mix = norm_mix[l][None, :]
        if l % 2 == 0:
            w_glr = jnp.pad(w_in_even_b[i, :, main_cols:], ((0, 0), (0, 128 - _GATE_RANK)))
            w_gate = jnp.pad(w_gate_up[i], ((0, 128 - _GATE_RANK), (0, 0))).astype(_BF16)
            proj, v32, loga = _inproj_even(x, g_mix, w_in_even_b, i, w_glr, w_gate, b_gate[i][None, :],
                                           tail_rows=rows_s)
            gout = g_gla_out[i][None, :]
            mix, gla_p = _mixer_even(
                proj, loga, w_spatial[i], bcol_p[i], gout, layer_slot=i, n_slots=n_even,
                total_rows=total, row_start=0, n_seqs=n_p, seq_len=t_p, state_prev=gla_p)
            mix, gla_s = _mixer_even(
                proj, loga, wmix_s[i], bcol_s[i], gout, layer_slot=i, n_slots=n_even,
                total_rows=total, row_start=rows_p, n_seqs=n_s, seq_len=t_s,
                state_in=state_gla, mix_prev=mix, state_prev=gla_s)
            v_rows.append(v32.reshape(n_s, t_s, a_width))
            x = _outproj_residual(mix, w_out_even_b, i, x)
        else:
            bg, z = _inproj_odd(x, g_mix, w_in_odd_b, i)
            buf = state_conv[i].astype(_F32)
            keep = _CONV_TAPS - 1
            hist1 = jnp.pad(buf[:, 1:2], ((0, 0), (0, t_s - 1), (0, 0))).reshape(rows_s, d)
            hist2 = jnp.pad(buf, ((0, 0), (0, t_s - keep), (0, 0))).reshape(rows_s, d)
            gated = _conv_gate(z, bg, conv_w[i], row_start=0, n_rows=rows_p, seq_len=t_p)
            gated = _conv_gate(z, bg, conv_w[i], row_start=rows_p, n_rows=rows_s, seq_len=t_s,
                               hist=(hist1, hist2), prev=gated)
            conv_p.append(jnp.stack([z[(n + 1) * t_p - keep:(n + 1) * t_p] for n in range(n_p)]))
            conv_s.append(z[rows_p:].reshape(n_s, t_s, d)[:, t_s - keep:])
            x = _outproj_residual(gated, w_out_odd_b, i, x)
        g_ffn = norm_ffn[l][None, :]
        if l < depth - 1:
            x = _ffn_residual(x, g_ffn, w_ffn_up, w_ffn_down, l)
        else:
            y_p, y_s = _ffn_residual(x, g_ffn, w_ffn_up, w_ffn_down, l,
                                     final_gain=norm_final[None, :], tail_rows=rows_s)

    return (y_p.reshape(n_p, t_p, d), y_s.reshape(n_s, t_s, d), gla_p, gla_s,
            jnp.stack(conv_p), jnp.stack(conv_s), jnp.stack(v_rows))
```

```python
import functools

import jax
import jax.numpy as jnp
from jax import lax
from jax.experimental import pallas as pl
from jax.experimental.pallas import tpu as pltpu

_F32 = jnp.float32
_BF16 = jnp.bfloat16

_EPS = 1e-6
_HEADS = 4
_A_HEAD = 256
_A_CHUNK = 128
_DK = 128
_DV = 256
_GATE_RANK = 16
_GATE_TAU = 16.0
_GLA_ROWS = 64
_CONV_TAPS = 3

_ROW_BLOCK = 1088
_COL_TILE = 512
_MIB = 1024 * 1024
_VMEM_LIMIT = 56 * _MIB
_VMEM_LIMIT_FFN = 62 * _MIB


def _params(semantics, vmem_limit=_VMEM_LIMIT):
    return pltpu.CompilerParams(dimension_semantics=semantics, vmem_limit_bytes=vmem_limit)


def _rmsnorm_rows(x, g):
    ms = jnp.mean(x * x, axis=-1, keepdims=True)
    return (x * lax.rsqrt(ms + _EPS)) * g


def _gelu_tanh(x):
    return x * (0.5 * (1.0 + jnp.tanh(0.7978845608028654 * (x + 0.044715 * (x * x * x)))))


def _log_sigmoid(x):
    return jnp.minimum(x, 0.0) - jnp.log(1.0 + jnp.exp(-jnp.abs(x)))


def _dot(a, b):
    return jnp.dot(a, b, preferred_element_type=_F32)


def _dot_nt(a, b):
    return lax.dot_general(a, b, (((1,), (1,)), ((), ())), preferred_element_type=_F32)


def _dot_tn(a, b):
    return lax.dot_general(a, b, (((0,), (0,)), ((), ())), preferred_element_type=_F32)


def _inproj_even_kernel(x_ref, g_ref, w_ref, wglr_ref, wgate_ref, bgate_ref,
                        proj_ref, v32_ref, loga_ref, hn_ref, *, gelu_tiles, v_tiles, tail_start):
    i, j = pl.program_id(0), pl.program_id(1)

    def project(hn, activate):
        acc = _dot(hn, w_ref[...])
        proj_ref[...] = (_gelu_tanh(acc) if activate else acc).astype(_BF16)

    @pl.when(j == 0)
    def _():
        hn = _rmsnorm_rows(x_ref[...], g_ref[...]).astype(_BF16)
        hn_ref[...] = hn
        glr = _dot(hn, wglr_ref[...])
        gate = _dot(glr.astype(_BF16), wgate_ref[...]) + bgate_ref[...]
        loga_ref[...] = _log_sigmoid(gate) * (1.0 / _GATE_TAU)
        project(hn, True)

    @pl.when((j > 0) & (j < gelu_tiles))
    def _():
        project(hn_ref[...], True)

    @pl.when(j >= gelu_tiles)
    def _():
        project(hn_ref[...], False)

    @pl.when((i == pl.num_programs(0) - 1) & (j >= v_tiles[0]) & (j < v_tiles[1]))
    def _():
        v32_ref[...] = _gelu_tanh(_dot(hn_ref[tail_start:, :], w_ref[...]))


def _inproj_even(x, g, w_all, layer, w_glr, w_gate, b_gate, tail_rows):
    m, d = x.shape
    n = w_all.shape[2] - _GATE_RANK
    bm, tn = _ROW_BLOCK, 2 * _COL_TILE
    assert tail_rows <= bm and (bm - tail_rows) % 16 == 0
    a_width = _HEADS * _A_HEAD
    gelu_tiles = 2 * a_width // tn
    v_tiles = (a_width // tn, 2 * a_width // tn)
    nk = w_gate.shape[1]

    def v32_map(i, j):
        tile = jnp.clip(j - v_tiles[0], 0, v_tiles[1] - v_tiles[0] - 1)
        return (0, jnp.where(i == m // bm - 1, tile, 0))

    return pl.pallas_call(
        functools.partial(_inproj_even_kernel, gelu_tiles=gelu_tiles, v_tiles=v_tiles,
                          tail_start=bm - tail_rows),
        grid=(m // bm, n // tn),
        in_specs=[
            pl.BlockSpec((bm, d), lambda i, j: (i, 0)),
            pl.BlockSpec((1, d), lambda i, j: (0, 0)),
            pl.BlockSpec((None, d, tn), lambda i, j: (layer, 0, j)),
            pl.BlockSpec(w_glr.shape, lambda i, j: (0, 0)),
            pl.BlockSpec(w_gate.shape, lambda i, j: (0, 0)),
            pl.BlockSpec((1, nk), lambda i, j: (0, 0)),
        ],
        out_specs=[
            pl.BlockSpec((bm, tn), lambda i, j: (i, j)),
            pl.BlockSpec((tail_rows, tn), v32_map),
            pl.BlockSpec((bm, nk), lambda i, j: (i, 0)),
        ],
        out_shape=[
            jax.ShapeDtypeStruct((m, n), _BF16),
            jax.ShapeDtypeStruct((tail_rows, a_width), _F32),
            jax.ShapeDtypeStruct((m, nk), _F32),
        ],
        scratch_shapes=[pltpu.VMEM((bm, d), _BF16)],
        compiler_params=_params(("arbitrary", "arbitrary")),
        name="inproj_even",
    )(x, g, w_all, w_glr, w_gate, b_gate)


def _inproj_odd_kernel(x_ref, g_ref, wb_ref, wc_ref, wh_ref, bg_ref, z_ref, hn_ref):
    j = pl.program_id(1)

    def project(hn):
        bg_ref[...] = _dot(hn, wb_ref[...]).astype(_BF16)
        z_ref[...] = _dot(hn, wc_ref[...]) * _dot(hn, wh_ref[...])

    @pl.when(j == 0)
    def _():
        hn = _rmsnorm_rows(x_ref[...], g_ref[...]).astype(_BF16)
        hn_ref[...] = hn
        project(hn)

    @pl.when(j > 0)
    def _():
        project(hn_ref[...])


def _inproj_odd(x, g, w, layer):
    m, d = x.shape
    c = w.shape[2] // 3
    bm, tn = _ROW_BLOCK, _COL_TILE
    nt = c // tn
    return pl.pallas_call(
        _inproj_odd_kernel,
        grid=(m // bm, nt),
        in_specs=[
            pl.BlockSpec((bm, d), lambda i, j: (i, 0)),
            pl.BlockSpec((1, d), lambda i, j: (0, 0)),
            pl.BlockSpec((None, d, tn), lambda i, j: (layer, 0, j)),
            pl.BlockSpec((None, d, tn), lambda i, j: (layer, 0, j + nt)),
            pl.BlockSpec((None, d, tn), lambda i, j: (layer, 0, j + 2 * nt)),
        ],
        out_specs=[
            pl.BlockSpec((bm, tn), lambda i, j: (i, j)),
            pl.BlockSpec((bm, tn), lambda i, j: (i, j)),
        ],
        out_shape=[
            jax.ShapeDtypeStruct((m, c), _BF16),
            jax.ShapeDtypeStruct((m, c), _F32),
        ],
        scratch_shapes=[pltpu.VMEM((bm, d), _BF16)],
        compiler_params=_params(("arbitrary", "arbitrary")),
        name="inproj_odd",
    )(x, g, w, w, w)


def _outproj_kernel(a_ref, w_ref, x_ref, o_ref):
    o_ref[...] = x_ref[...] + _dot(a_ref[...], w_ref[...])


def _outproj_residual(a, w, layer, x):
    m, k = a.shape
    n = w.shape[2]
    bm = _ROW_BLOCK // 2
    return pl.pallas_call(
        _outproj_kernel,
        grid=(m // bm,),
        in_specs=[
            pl.BlockSpec((bm, k), lambda i: (i, 0)),
            pl.BlockSpec((None, k, n), lambda i: (layer, 0, 0), pipeline_mode=pl.Buffered(1)),
            pl.BlockSpec((bm, n), lambda i: (i, 0)),
        ],
        out_specs=pl.BlockSpec((bm, n), lambda i: (i, 0)),
        out_shape=jax.ShapeDtypeStruct((m, n), _F32),
        compiler_params=_params(("arbitrary",)),
        name="outproj_residual",
    )(a, w, x)


def _ffn_kernel(x_ref, g_ref, wup_ref, wdn_ref, *rest, tail_start):
    final = len(rest) == 4
    if final:
        gfin_ref, o_ref, tail_ref, hn_ref = rest
    else:
        o_ref, hn_ref = rest
    i, j = pl.program_id(0), pl.program_id(1)

    def mlp_tile(hn):
        h = _dot(hn, wup_ref[...].astype(_BF16))
        h = jnp.square(jnp.maximum(h, 0.0)).astype(_BF16)
        return _dot(h, wdn_ref[...].astype(_BF16))

    @pl.when(j == 0)
    def _():
        x = x_ref[...]
        hn = _rmsnorm_rows(x, g_ref[...]).astype(_BF16)
        hn_ref[...] = hn
        o_ref[...] = x + mlp_tile(hn)

    @pl.when(j > 0)
    def _():
        o_ref[...] += mlp_tile(hn_ref[...])

    if final:
        @pl.when(j == pl.num_programs(1) - 1)
        def _():
            o_ref[...] = _rmsnorm_rows(o_ref[...], gfin_ref[...])

            @pl.when(i == pl.num_programs(0) - 1)
            def _():
                tail_ref[...] = o_ref[tail_start:, :]


def _ffn_residual(x, g, w_up, w_down, layer, final_gain=None, tail_rows=0):
    m, d = x.shape
    f = w_up.shape[2]
    bm, tf = _ROW_BLOCK, _COL_TILE
    in_specs = [
        pl.BlockSpec((bm, d), lambda i, j: (i, 0)),
        pl.BlockSpec((1, d), lambda i, j: (0, 0)),
        pl.BlockSpec((None, d, tf), lambda i, j: (layer, 0, j)),
        pl.BlockSpec((None, tf, d), lambda i, j: (layer, j, 0)),
    ]
    args = [x, g, w_up, w_down]
    row_block = pl.BlockSpec((bm, d), lambda i, j: (i, 0))
    if final_gain is None:
        out_specs = row_block
        out_shape = jax.ShapeDtypeStruct((m, d), _F32)
    else:
        assert tail_rows <= bm and (bm - tail_rows) % 8 == 0
        in_specs.append(pl.BlockSpec((1, d), lambda i, j: (0, 0)))
        args.append(final_gain)
        out_specs = [row_block, pl.BlockSpec((tail_rows, d), lambda i, j: (0, 0))]
        out_shape = [jax.ShapeDtypeStruct((m - tail_rows, d), _F32),
                     jax.ShapeDtypeStruct((tail_rows, d), _F32)]
    return pl.pallas_call(
        functools.partial(_ffn_kernel, tail_start=bm - tail_rows),
        grid=(m // bm, f // tf),
        in_specs=in_specs,
        out_specs=out_specs,
        out_shape=out_shape,
        scratch_shapes=[pltpu.VMEM((bm, d), _BF16)],
        compiler_params=_params(("arbitrary", "arbitrary"), _VMEM_LIMIT_FFN),
        name="ffn_residual" if final_gain is None else "ffn_final",
    )(*args)


def _split3(x):
    hi = x.astype(_BF16)
    r1 = x - hi.astype(_F32)
    mid = r1.astype(_BF16)
    lo = (r1 - mid.astype(_F32)).astype(_BF16)
    return hi, mid, lo


def _mixer_even_kernel(*refs, rows, a_chunk, a_seq, seqs_per_group, carry_state):
    (u_ref, v_ref, q_ref, k_ref, vb_ref, r_ref, la_ref, wmix_ref, bcol_ref, gout_ref) = refs[:10]
    rest = refs[10:]
    if carry_state:
        mix_ref, sout_ref = rest[-2:]
        sin_ref = None
    else:
        sin_ref = rest[0]
        mix_ref, sout_ref = rest[-2:]

    tt = lax.broadcasted_iota(jnp.int32, (a_chunk, a_chunk), 0)
    ss = lax.broadcasted_iota(jnp.int32, (a_chunk, a_chunk), 1)
    a_mask = (ss <= tt) & ((tt // a_seq) == (ss // a_seq))
    for h in range(_HEADS):
        cols = slice(h * _A_HEAD, (h + 1) * _A_HEAD)
        wm = jnp.where(a_mask, wmix_ref[h], 0.0).astype(_BF16)
        bias = bcol_ref[h]
        for c in range(rows // a_chunk):
            rws = slice(c * a_chunk, (c + 1) * a_chunk)
            mixed = _dot(wm, v_ref[rws, cols]) + bias
            mix_ref[rws, cols] = (u_ref[rws, cols].astype(_F32) * mixed).astype(_BF16)

    gr = _GLA_ROWS
    seq_rows = gr // seqs_per_group
    ti = lax.broadcasted_iota(jnp.int32, (gr, gr), 0)
    si = lax.broadcasted_iota(jnp.int32, (gr, gr), 1)
    same_seq = (ti // seq_rows) == (si // seq_rows)
    causal = (si <= ti) & same_seq
    cum_and_total = jnp.concatenate(
        [jnp.where(causal, 1.0, 0.0), jnp.where(same_seq, 1.0, 0.0)], axis=0).astype(_BF16)
    kw = _HEADS * _DK
    row_k = lax.broadcasted_iota(jnp.int32, (gr, _DK), 0)
    row_v = lax.broadcasted_iota(jnp.int32, (gr, _DV), 0)
    pos_all = lax.broadcasted_iota(jnp.int32, (gr, kw), 0) % seq_rows
    ones_v = jnp.ones((gr, _DV), _BF16)
    scale = _DK ** -0.5
    a_width = _HEADS * _A_HEAD

    if carry_state:
        @pl.when(pl.program_id(1) == 0)
        def _():
            sout_ref[...] = jnp.zeros(sout_ref.shape, _F32)
        states = [sout_ref[h] for h in range(_HEADS)]

    for gi in range(rows // gr):
        rws = slice(gi * gr, (gi + 1) * gr)
        la = la_ref[rws, :]
        la_hi = la.astype(_BF16)
        la_lo = (la - la_hi.astype(_F32)).astype(_BF16)
        sums = _dot(cum_and_total, jnp.concatenate([la_hi, la_lo], axis=1))
        g_cum = sums[:gr, :kw] + sums[:gr, kw:]
        g_tot = sums[gr:, :kw] + sums[gr:, kw:]
        q = q_ref[rws, :].astype(_F32) * scale
        k = k_ref[rws, :].astype(_F32)
        q_in_all = (q * jnp.exp(g_cum)).astype(_BF16)
        k_in_all = (k * jnp.exp(-g_cum)).astype(_BF16)
        k_dec_all = (k * jnp.exp(g_tot - g_cum)).astype(_BF16)
        t_hi, t_mid, t_lo = _split3(g_tot)
        pieces_all = jnp.where(pos_all == 0, t_hi,
                               jnp.where(pos_all == 1, t_mid,
                                         jnp.where(pos_all == 2, t_lo, jnp.zeros_like(t_hi))))
        for h in range(_HEADS):
            kc = slice(h * _DK, (h + 1) * _DK)
            vc = slice(h * _DV, (h + 1) * _DV)
            q_in, k_in, k_dec, pieces = q_in_all[:, kc], k_in_all[:, kc], k_dec_all[:, kc], pieces_all[:, kc]
            v = vb_ref[rws, vc]
            scores = jnp.where(causal, _dot_nt(q_in, k_in), 0.0).astype(_BF16)
            o = _dot(scores, v)
            if carry_state:
                o = o + _dot(q_in, states[h].astype(_BF16))
                decay = jnp.exp(_dot_tn(pieces, ones_v))
                states[h] = states[h] * decay + _dot_tn(k_dec, v)
            else:
                zero = jnp.zeros_like(pieces)
                for s in range(seqs_per_group):
                    in_seq_k = (row_k // seq_rows) == s
                    in_seq_v = (row_v // seq_rows) == s
                    s0 = sin_ref[s, h]
                    o = o + jnp.where(in_seq_v, _dot(q_in, s0.astype(_BF16)), 0.0)
                    decay = jnp.exp(_dot_tn(jnp.where(in_seq_k, pieces, zero), ones_v))
                    kd = jnp.where(in_seq_k, k_dec, zero)
                    sout_ref[s, h] = s0 * decay + _dot_tn(kd, v)
            ms = jnp.mean(o * o, axis=-1, keepdims=True)
            on = (o * lax.rsqrt(ms + _EPS)) * gout_ref[:, vc]
            r = r_ref[rws, vc].astype(_F32)
            gated = on * (r * (1.0 / (1.0 + jnp.exp(-r))))
            mix_ref[rws, a_width + h * _DV:a_width + (h + 1) * _DV] = gated.astype(_BF16)

    if carry_state:
        for h in range(_HEADS):
            sout_ref[h] = states[h]


def _mixer_even(proj, loga, wmix, bcol, gout, *, layer_slot, n_slots, total_rows,
                row_start, n_seqs, seq_len, state_in=None, mix_prev=None, state_prev=None):
    carry = state_in is None
    a_width = _HEADS * _A_HEAD
    kw = _HEADS * _DK
    if carry:
        rows = 256
        blocks_per_seq = seq_len // rows
        grid = (n_seqs, blocks_per_seq)
        first = row_start // rows
        rmap = lambda col: (lambda n, c: (first + n * blocks_per_seq + c, col))
        a_chunk, a_seq, spg = _A_CHUNK, _A_CHUNK, 1
        state_shape = (n_slots, n_seqs, _HEADS, _DK, _DV)
        state_spec = pl.BlockSpec((None, None, _HEADS, _DK, _DV),
                                  lambda n, c: (layer_slot, n, 0, 0, 0))
        sem = ("arbitrary", "arbitrary")
    else:
        rows = _GLA_ROWS
        spg = rows // seq_len
        grid = (n_seqs // spg,)
        first = row_start // rows
        rmap = lambda col: (lambda g: (first + g, col))
        a_chunk, a_seq = rows, seq_len
        state_shape = (n_slots, n_seqs, _HEADS, _DK, _DV)
        state_spec = pl.BlockSpec((None, spg, _HEADS, _DK, _DV),
                                  lambda g: (layer_slot, g, 0, 0, 0))
        sem = ("arbitrary",)
    const = (lambda *_: (0, 0, 0))
    in_specs = [
        pl.BlockSpec((rows, a_width), rmap(0)),
        pl.BlockSpec((rows, a_width), rmap(1)),
        pl.BlockSpec((rows, kw), rmap(2 * a_width // kw)),
        pl.BlockSpec((rows, kw), rmap(2 * a_width // kw + 1)),
        pl.BlockSpec((rows, a_width), rmap(3)),
        pl.BlockSpec((rows, a_width), rmap(4)),
        pl.BlockSpec((rows, kw), rmap(0)),
        pl.BlockSpec(wmix.shape, const),
        pl.BlockSpec(bcol.shape, const),
        pl.BlockSpec(gout.shape, lambda *_: (0, 0)),
    ]
    args = [proj, proj, proj, proj, proj, proj, loga, wmix, bcol, gout]
    if not carry:
        in_specs.append(state_spec)
        args.append(state_in)
    aliases = {}
    if mix_prev is not None:
        aliases[len(args)] = 0
        in_specs.append(pl.BlockSpec(memory_space=pl.ANY))
        args.append(mix_prev)
    if state_prev is not None:
        aliases[len(args)] = 1
        in_specs.append(pl.BlockSpec(memory_space=pl.ANY))
        args.append(state_prev)
    return pl.pallas_call(
        functools.partial(_mixer_even_kernel, rows=rows, a_chunk=a_chunk, a_seq=a_seq,
                          seqs_per_group=spg, carry_state=carry),
        grid=grid,
        in_specs=in_specs,
        out_specs=[pl.BlockSpec((rows, 2 * a_width), rmap(0)), state_spec],
        out_shape=[jax.ShapeDtypeStruct((total_rows, 2 * a_width), _BF16),
                   jax.ShapeDtypeStruct(state_shape, _F32)],
        input_output_aliases=aliases,
        compiler_params=_params(sem),
        name="mixer_even_long" if carry else "mixer_even_short",
    )(*args)


def _conv_gate_kernel(*refs, rows, seq_len, has_halo, has_hist):
    refs = list(refs)
    z_ref = refs.pop(0)
    halo_ref = refs.pop(0) if has_halo else None
    bg_ref = refs.pop(0)
    cw_ref = refs.pop(0)
    h1_ref = refs.pop(0) if has_hist else None
    h2_ref = refs.pop(0) if has_hist else None
    out_ref, zs_ref = refs[-2:]
    cols = z_ref.shape[1]

    z = z_ref[...]
    zs_ref[8:8 + rows, :] = z
    zs_ref[0:8, :] = halo_ref[...] if has_halo else jnp.zeros((8, cols), _F32)
    prev1 = zs_ref[pl.ds(7, rows), :]
    prev2 = zs_ref[pl.ds(6, rows), :]
    t = (pl.program_id(0) * rows + lax.broadcasted_iota(jnp.int32, (rows, cols), 0)) % seq_len
    hist1 = h1_ref[...] if has_hist else 0.0
    hist2 = h2_ref[...] if has_hist else 0.0
    prev1 = jnp.where(t >= 1, prev1, hist1)
    prev2 = jnp.where(t >= 2, prev2, hist2)
    conv = prev2 * cw_ref[0:1, :] + prev1 * cw_ref[1:2, :] + z * cw_ref[2:3, :]
    out_ref[...] = (bg_ref[...].astype(_F32) * conv).astype(_BF16)


def _conv_gate(z, bg, cw, *, row_start, n_rows, seq_len, hist=None, prev=None):
    total_rows, c = z.shape
    rows, tn = 512, 1024
    first = row_start // rows
    has_halo = seq_len > rows
    has_hist = hist is not None
    main = lambda i, j: (first + i, j)
    in_specs = [pl.BlockSpec((rows, tn), main)]
    args = [z]
    if has_halo:
        in_specs.append(pl.BlockSpec(
            (8, tn), lambda i, j: (jnp.maximum((first + i) * (rows // 8) - 1, 0), j)))
        args.append(z)
    in_specs += [pl.BlockSpec((rows, tn), main), pl.BlockSpec((_CONV_TAPS, tn), lambda i, j: (0, j))]
    args += [bg, cw]
    if has_hist:
        in_specs += [pl.BlockSpec((rows, tn), lambda i, j: (i, j))] * 2
        args += list(hist)
    aliases = {}
    if prev is not None:
        aliases[len(args)] = 0
        in_specs.append(pl.BlockSpec(memory_space=pl.ANY))
        args.append(prev)
    return pl.pallas_call(
        functools.partial(_conv_gate_kernel, rows=rows, seq_len=seq_len,
                          has_halo=has_halo, has_hist=has_hist),
        grid=(n_rows // rows, c // tn),
        in_specs=in_specs,
        out_specs=pl.BlockSpec((rows, tn), main),
        out_shape=jax.ShapeDtypeStruct((total_rows, c), _BF16),
        scratch_shapes=[pltpu.VMEM((rows + 8, tn), _F32)],
        input_output_aliases=aliases,
        compiler_params=_params(("arbitrary", "arbitrary")),
        name="conv_gate_long" if has_halo else "conv_gate_short",
    )(*args)


def kernel(x_prompt, x_sample, state_gla, state_conv, norm_mix, norm_ffn, norm_final, w_in_even, w_gate_up, b_gate, w_spatial, b_spatial, g_gla_out, w_out_even, w_in_odd, conv_w, w_out_odd, w_ffn_up, w_ffn_down):
    n_p, t_p, d = x_prompt.shape
    n_s, t_s, _ = x_sample.shape
    rows_p, rows_s = n_p * t_p, n_s * t_s
    total = rows_p + rows_s
    depth = norm_mix.shape[0]
    n_even = w_in_even.shape[0]
    a_width = _HEADS * _A_HEAD
    main_cols = w_in_even.shape[2] - _GATE_RANK

    x = jnp.concatenate([x_prompt.reshape(rows_p, d), x_sample.reshape(rows_s, d)], axis=0)

    reps = _GLA_ROWS // t_s
    wmix_s = jnp.tile(w_spatial[:, :, :t_s, :t_s], (1, 1, reps, reps))
    bcol_s = jnp.tile(b_spatial[:, :, :t_s], (1, 1, reps))[..., None]
    bcol_p = b_spatial[..., None]

    w_in_even_b = w_in_even.astype(_BF16)
    w_in_odd_b = w_in_odd.astype(_BF16)
    w_out_even_b = w_out_even.astype(_BF16)
    w_out_odd_b = w_out_odd.astype(_BF16)

    v_rows, conv_p, conv_s = [], [], []
    gla_p = gla_s = None
    for l in range(depth):
        i = l // 2
        g_mix = norm_mix[l][None, :]
        if l % 2 == 0:
            w_glr = jnp.pad(w_in_even_b[i, :, main_cols:], ((0, 0), (0, 128 - _GATE_RANK)))
            w_gate = jnp.pad(w_gate_up[i], ((0, 128 - _GATE_RANK), (0, 0))).astype(_BF16)
            proj, v32, loga = _inproj_even(x, g_mix, w_in_even_b, i, w_glr, w_gate, b_gate[i][None, :],
                                           tail_rows=rows_s)
            gout = g_gla_out[i][None, :]
            mix, gla_p = _mixer_even(
                proj, loga, w_spatial[i], bcol_p[i], gout, layer_slot=i, n_slots=n_even,
                total_rows=total, row_start=0, n_seqs=n_p, seq_len=t_p, state_prev=gla_p)
            mix, gla_s = _mixer_even(
                proj, loga, wmix_s[i], bcol_s[i], gout, layer_slot=i, n_slots=n_even,
                total_rows=total, row_start=rows_p, n_seqs=n_s, seq_len=t_s,
                state_in=state_gla, mix_prev=mix, state_prev=gla_s)
            v_rows.append(v32.reshape(n_s, t_s, a_width))
            x = _outproj_residual(mix, w_out_even_b, i, x)
        else:
            bg, z = _inproj_odd(x, g_mix, w_in_odd_b, i)
            buf = state_conv[i].astype(_F32)
            keep = _CONV_TAPS - 1
            hist1 = jnp.pad(buf[:, 1:2], ((0, 0), (0, t_s - 1), (0, 0))).reshape(rows_s, d)
            hist2 = jnp.pad(buf, ((0, 0), (0, t_s - keep), (0, 0))).reshape(rows_s, d)
            gated = _conv_gate(z, bg, conv_w[i], row_start=0, n_rows=rows_p, seq_len=t_p)
            gated = _conv_gate(z, bg, conv_w[i], row_start=rows_p, n_rows=rows_s, seq_len=t_s,
                               hist=(hist1, hist2), prev=gated)
            conv_p.append(jnp.stack([z[(n + 1) * t_p - keep:(n + 1) * t_p] for n in range(n_p)]))
            conv_s.append(z[rows_p:].reshape(n_s, t_s, d)[:, t_s - keep:])
            x = _outproj_residual(gated, w_out_odd_b, i, x)
        g_ffn = norm_ffn[l][None, :]
        if l < depth - 1:
            x = _ffn_residual(x, g_ffn, w_ffn_up, w_ffn_down, l)
        else:
            y_p, y_s = _ffn_residual(x, g_ffn, w_ffn_up, w_ffn_down, l,
                                     final_gain=norm_final[None, :], tail_rows=rows_s)

    return (y_p.reshape(n_p, t_p, d), y_s.reshape(n_s, t_s, d), gla_p, gla_s,
            jnp.stack(conv_p), jnp.stack(conv_s), jnp.stack(v_rows))
```

```python
import functools

import jax
import jax.numpy as jnp
from jax import lax
from jax.experimental import pallas as pl
from jax.experimental.pallas import tpu as pltpu

_F32 = jnp.float32
_BF16 = jnp.bfloat16

_EPS = 1e-6
_HEADS = 4
_A_HEAD = 256
_A_CHUNK = 128
_DK = 128
_DV = 256
_GATE_RANK = 16
_GATE_TAU = 16.0
_GLA_ROWS = 64
_CONV_TAPS = 3

_ROW_BLOCK = 1088
_COL_TILE = 512
_MIB = 1024 * 1024
_VMEM_LIMIT = 56 * _MIB
_VMEM_LIMIT_FFN = 62 * _MIB


def _params(semantics, vmem_limit=_VMEM_LIMIT):
    return pltpu.CompilerParams(dimension_semantics=semantics, vmem_limit_bytes=vmem_limit)


def _rmsnorm_rows(x, g):
    ms = jnp.mean(x * x, axis=-1, keepdims=True)
    return (x * lax.rsqrt(ms + _EPS)) * g


def _gelu_tanh(x):
    return x * (0.5 * (1.0 + jnp.tanh(0.7978845608028654 * (x + 0.044715 * (x * x * x)))))


def _log_sigmoid(x):
    return jnp.minimum(x, 0.0) - jnp.log(1.0 + jnp.exp(-jnp.abs(x)))


def _dot(a, b):
    return jnp.dot(a, b, preferred_element_type=_F32)


def _dot_nt(a, b):
    return lax.dot_general(a, b, (((1,), (1,)), ((), ())), preferred_element_type=_F32)


def _dot_tn(a, b):
    return lax.dot_general(a, b, (((0,), (0,)), ((), ())), preferred_element_type=_F32)


def _inproj_even_kernel(x_ref, g_ref, w_ref, wglr_ref, wgate_ref, bgate_ref,
                        proj_ref, v32_ref, loga_ref, hn_ref, *, gelu_tiles, v_tiles, tail_start):
    i, j = pl.program_id(0), pl.program_id(1)

    def project(hn, activate):
        acc = _dot(hn, w_ref[...])
        proj_ref[...] = (_gelu_tanh(acc) if activate else acc).astype(_BF16)

    @pl.when(j == 0)
    def _():
        hn = _rmsnorm_rows(x_ref[...], g_ref[...]).astype(_BF16)
        hn_ref[...] = hn
        glr = _dot(hn, wglr_ref[...])
        gate = _dot(glr.astype(_BF16), wgate_ref[...]) + bgate_ref[...]
        loga_ref[...] = _log_sigmoid(gate) * (1.0 / _GATE_TAU)
        project(hn, True)

    @pl.when((j > 0) & (j < gelu_tiles))
    def _():
        project(hn_ref[...], True)

    @pl.when(j >= gelu_tiles)
    def _():
        project(hn_ref[...], False)

    @pl.when((i == pl.num_programs(0) - 1) & (j >= v_tiles[0]) & (j < v_tiles[1]))
    def _():
        v32_ref[...] = _gelu_tanh(_dot(hn_ref[tail_start:, :], w_ref[...]))


def _inproj_even(x, g, w_all, layer, w_glr, w_gate, b_gate, tail_rows):
    m, d = x.shape
    n = w_all.shape[2] - _GATE_RANK
    bm, tn = _ROW_BLOCK, 2 * _COL_TILE
    assert tail_rows <= bm and (bm - tail_rows) % 16 == 0
    a_width = _HEADS * _A_HEAD
    gelu_tiles = 2 * a_width // tn
    v_tiles = (a_width // tn, 2 * a_width // tn)
    nk = w_gate.shape[1]

    def v32_map(i, j):
        tile = jnp.clip(j - v_tiles[0], 0, v_tiles[1] - v_tiles[0] - 1)
        return (0, jnp.where(i == m // bm - 1, tile, 0))

    return pl.pallas_call(
        functools.partial(_inproj_even_kernel, gelu_tiles=gelu_tiles, v_tiles=v_tiles,
                          tail_start=bm - tail_rows),
        grid=(m // bm, n // tn),
        in_specs=[
            pl.BlockSpec((bm, d), lambda i, j: (i, 0)),
            pl.BlockSpec((1, d), lambda i, j: (0, 0)),
            pl.BlockSpec((None, d, tn), lambda i, j: (layer, 0, j)),
            pl.BlockSpec(w_glr.shape, lambda i, j: (0, 0)),
            pl.BlockSpec(w_gate.shape, lambda i, j: (0, 0)),
            pl.BlockSpec((1, nk), lambda i, j: (0, 0)),
        ],
        out_specs=[
            pl.BlockSpec((bm, tn), lambda i, j: (i, j)),
            pl.BlockSpec((tail_rows, tn), v32_map),
            pl.BlockSpec((bm, nk), lambda i, j: (i, 0)),
        ],
        out_shape=[
            jax.ShapeDtypeStruct((m, n), _BF16),
            jax.ShapeDtypeStruct((tail_rows, a_width), _F32),
            jax.ShapeDtypeStruct((m, nk), _F32),
        ],
        scratch_shapes=[pltpu.VMEM((bm, d), _BF16)],
        compiler_params=_params(("arbitrary", "arbitrary")),
        name="inproj_even",
    )(x, g, w_all, w_glr, w_gate, b_gate)


def _inproj_odd_kernel(x_ref, g_ref, wb_ref, wc_ref, wh_ref, bg_ref, z_ref, hn_ref):
    j = pl.program_id(1)

    def project(hn):
        bg_ref[...] = _dot(hn, wb_ref[...]).astype(_BF16)
        z_ref[...] = _dot(hn, wc_ref[...]) * _dot(hn, wh_ref[...])

    @pl.when(j == 0)
    def _():
        hn = _rmsnorm_rows(x_ref[...], g_ref[...]).astype(_BF16)
        hn_ref[...] = hn
        project(hn)

    @pl.when(j > 0)
    def _():
        project(hn_ref[...])


def _inproj_odd(x, g, w, layer):
    m, d = x.shape
    c = w.shape[2] // 3
    bm, tn = _ROW_BLOCK, _COL_TILE
    nt = c // tn
    return pl.pallas_call(
        _inproj_odd_kernel,
        grid=(m // bm, nt),
        in_specs=[
            pl.BlockSpec((bm, d), lambda i, j: (i, 0)),
            pl.BlockSpec((1, d), lambda i, j: (0, 0)),
            pl.BlockSpec((None, d, tn), lambda i, j: (layer, 0, j)),
            pl.BlockSpec((None, d, tn), lambda i, j: (layer, 0, j + nt)),
            pl.BlockSpec((None, d, tn), lambda i, j: (layer, 0, j + 2 * nt)),
        ],
        out_specs=[
            pl.BlockSpec((bm, tn), lambda i, j: (i, j)),
            pl.BlockSpec((bm, tn), lambda i, j: (i, j)),
        ],
        out_shape=[
            jax.ShapeDtypeStruct((m, c), _BF16),
            jax.ShapeDtypeStruct((m, c), _F32),
        ],
        scratch_shapes=[pltpu.VMEM((bm, d), _BF16)],
        compiler_params=_params(("arbitrary", "arbitrary")),
        name="inproj_odd",
    )(x, g, w, w, w)


def _outproj_kernel(a_ref, w_ref, x_ref, o_ref):
    o_ref[...] = x_ref[...] + _dot(a_ref[...], w_ref[...])


def _outproj_residual(a, w, layer, x):
    m, k = a.shape
    n = w.shape[2]
    bm = _ROW_BLOCK // 2
    return pl.pallas_call(
        _outproj_kernel,
        grid=(m // bm,),
        in_specs=[
            pl.BlockSpec((bm, k), lambda i: (i, 0)),
            pl.BlockSpec((None, k, n), lambda i: (layer, 0, 0), pipeline_mode=pl.Buffered(1)),
            pl.BlockSpec((bm, n), lambda i: (i, 0)),
        ],
        out_specs=pl.BlockSpec((bm, n), lambda i: (i, 0)),
        out_shape=jax.ShapeDtypeStruct((m, n), _F32),
        compiler_params=_params(("arbitrary",)),
        name="outproj_residual",
    )(a, w, x)


def _ffn_kernel(x_ref, g_ref, wup_ref, wdn_ref, *rest, tail_start):
    final = len(rest) == 4
    if final:
        gfin_ref, o_ref, tail_ref, hn_ref = rest
    else:
        o_ref, hn_ref = rest
    i, j = pl.program_id(0), pl.program_id(1)

    def mlp_tile(hn):
        h = _dot(hn, wup_ref[...].astype(_BF16))
        h = jnp.square(jnp.maximum(h, 0.0)).astype(_BF16)
        return _dot(h, wdn_ref[...].astype(_BF16))

    @pl.when(j == 0)
    def _():
        x = x_ref[...]
        hn = _rmsnorm_rows(x, g_ref[...]).astype(_BF16)
        hn_ref[...] = hn
        o_ref[...] = x + mlp_tile(hn)

    @pl.when(j > 0)
    def _():
        o_ref[...] += mlp_tile(hn_ref[...])

    if final:
        @pl.when(j == pl.num_programs(1) - 1)
        def _():
            o_ref[...] = _rmsnorm_rows(o_ref[...], gfin_ref[...])

            @pl.when(i == pl.num_programs(0) - 1)
            def _():
                tail_ref[...] = o_ref[tail_start:, :]


def _ffn_residual(x, g, w_up, w_down, layer, final_gain=None, tail_rows=0):
    m, d = x.shape
    f = w_up.shape[2]
    bm, tf = _ROW_BLOCK, _COL_TILE
    in_specs = [
        pl.BlockSpec((bm, d), lambda i, j: (i, 0)),
        pl.BlockSpec((1, d), lambda i, j: (0, 0)),
        pl.BlockSpec((None, d, tf), lambda i, j: (layer, 0, j)),
        pl.BlockSpec((None, tf, d), lambda i, j: (layer, j, 0)),
    ]
    args = [x, g, w_up, w_down]
    row_block = pl.BlockSpec((bm, d), lambda i, j: (i, 0))
    if final_gain is None:
        out_specs = row_block
        out_shape = jax.ShapeDtypeStruct((m, d), _F32)
    else:
        assert tail_rows <= bm and (bm - tail_rows) % 8 == 0
        in_specs.append(pl.BlockSpec((1, d), lambda i, j: (0, 0)))
        args.append(final_gain)
        out_specs = [row_block, pl.BlockSpec((tail_rows, d), lambda i, j: (0, 0))]
        out_shape = [jax.ShapeDtypeStruct((m - tail_rows, d), _F32),
                     jax.ShapeDtypeStruct((tail_rows, d), _F32)]
    return pl.pallas_call(
        functools.partial(_ffn_kernel, tail_start=bm - tail_rows),
        grid=(m // bm, f // tf),
        in_specs=in_specs,
        out_specs=out_specs,
        out_shape=out_shape,
        scratch_shapes=[pltpu.VMEM((bm, d), _BF16)],
        compiler_params=_params(("arbitrary", "arbitrary"), _VMEM_LIMIT_FFN),
        name="ffn_residual" if final_gain is None else "ffn_final",
    )(*args)


def _split3(x):
    hi = x.astype(_BF16)
    r1 = x - hi.astype(_F32)
    mid = r1.astype(_BF16)
    lo = (r1 - mid.astype(_F32)).astype(_BF16)
    return hi, mid, lo


def _mixer_even_kernel(*refs, rows, a_chunk, a_seq, seqs_per_group, carry_state):
    (u_ref, v_ref, q_ref, k_ref, vb_ref, r_ref, la_ref, wmix_ref, bcol_ref, gout_ref) = refs[:10]
    rest = refs[10:]
    if carry_state:
        mix_ref, sout_ref = rest[-2:]
        sin_ref = None
    else:
        sin_ref = rest[0]
        mix_ref, sout_ref = rest[-2:]

    tt = lax.broadcasted_iota(jnp.int32, (a_chunk, a_chunk), 0)
    ss = lax.broadcasted_iota(jnp.int32, (a_chunk, a_chunk), 1)
    a_mask = (ss <= tt) & ((tt // a_seq) == (ss // a_seq))
    for h in range(_HEADS):
        cols = slice(h * _A_HEAD, (h + 1) * _A_HEAD)
        wm = jnp.where(a_mask, wmix_ref[h], 0.0).astype(_BF16)
        bias = bcol_ref[h]
        for c in range(rows // a_chunk):
            rws = slice(c * a_chunk, (c + 1) * a_chunk)
            mixed = _dot(wm, v_ref[rws, cols]) + bias
            mix_ref[rws, cols] = (u_ref[rws, cols].astype(_F32) * mixed).astype(_BF16)

    gr = _GLA_ROWS
    seq_rows = gr // seqs_per_group
    ti = lax.broadcasted_iota(jnp.int32, (gr, gr), 0)
    si = lax.broadcasted_iota(jnp.int32, (gr, gr), 1)
    same_seq = (ti // seq_rows) == (si // seq_rows)
    causal = (si <= ti) & same_seq
    cum_and_total = jnp.concatenate(
        [jnp.where(causal, 1.0, 0.0), jnp.where(same_seq, 1.0, 0.0)], axis=0).astype(_BF16)
    kw = _HEADS * _DK
    row_k = lax.broadcasted_iota(jnp.int32, (gr, _DK), 0)
    row_v = lax.broadcasted_iota(jnp.int32, (gr, _DV), 0)
    pos_all = lax.broadcasted_iota(jnp.int32, (gr, kw), 0) % seq_rows
    ones_v = jnp.ones((gr, _DV), _BF16)
    scale = _DK ** -0.5
    a_width = _HEADS * _A_HEAD

    if carry_state:
        @pl.when(pl.program_id(1) == 0)
        def _():
            sout_ref[...] = jnp.zeros(sout_ref.shape, _F32)
        states = [sout_ref[h] for h in range(_HEADS)]

    for gi in range(rows // gr):
        rws = slice(gi * gr, (gi + 1) * gr)
        la = la_ref[rws, :]
        la_hi = la.astype(_BF16)
        la_lo = (la - la_hi.astype(_F32)).astype(_BF16)
        sums = _dot(cum_and_total, jnp.concatenate([la_hi, la_lo], axis=1))
        g_cum = sums[:gr, :kw] + sums[:gr, kw:]
        g_tot = sums[gr:, :kw] + sums[gr:, kw:]
        q = q_ref[rws, :].astype(_F32) * scale
        k = k_ref[rws, :].astype(_F32)
        q_in_all = (q * jnp.exp(g_cum)).astype(_BF16)
        k_in_all = (k * jnp.exp(-g_cum)).astype(_BF16)
        k_dec_all = (k * jnp.exp(g_tot - g_cum)).astype(_BF16)
        t_hi, t_mid, t_lo = _split3(g_tot)
        pieces_all = jnp.where(pos_all == 0, t_hi,
                               jnp.where(pos_all == 1, t_mid,
                                         jnp.where(pos_all == 2, t_lo, jnp.zeros_like(t_hi))))
        for h in range(_HEADS):
            kc = slice(h * _DK, (h + 1) * _DK)
            vc = slice(h * _DV, (h + 1) * _DV)
            q_in, k_in, k_dec, pieces = q_in_all[:, kc], k_in_all[:, kc], k_dec_all[:, kc], pieces_all[:, kc]
            v = vb_ref[rws, vc]
            scores = jnp.where(causal, _dot_nt(q_in, k_in), 0.0).astype(_BF16)
            if carry_state:
                o = _dot(jnp.concatenate([q_in, scores], axis=1),
                         jnp.concatenate([states[h].astype(_BF16), v], axis=0))
                decay = jnp.exp(_dot_tn(pieces, ones_v))
                states[h] = states[h] * decay + _dot_tn(k_dec, v)
            else:
                o = _dot(scores, v)
                zero = jnp.zeros_like(pieces)
                for s in range(seqs_per_group):
                    in_seq_k = (row_k // seq_rows) == s
                    in_seq_v = (row_v // seq_rows) == s
                    s0 = sin_ref[s, h]
                    o = o + jnp.where(in_seq_v, _dot(q_in, s0.astype(_BF16)), 0.0)
                    decay = jnp.exp(_dot_tn(jnp.where(in_seq_k, pieces, zero), ones_v))
                    kd = jnp.where(in_seq_k, k_dec, zero)
                    sout_ref[s, h] = s0 * decay + _dot_tn(kd, v)
            ms = jnp.mean(o * o, axis=-1, keepdims=True)
            on = (o * lax.rsqrt(ms + _EPS)) * gout_ref[:, vc]
            r = r_ref[rws, vc].astype(_F32)
            gated = on * (r * (1.0 / (1.0 + jnp.exp(-r))))
            mix_ref[rws, a_width + h * _DV:a_width + (h + 1) * _DV] = gated.astype(_BF16)

    if carry_state:
        for h in range(_HEADS):
            sout_ref[h] = states[h]


def _mixer_even(proj, loga, wmix, bcol, gout, *, layer_slot, n_slots, total_rows,
                row_start, n_seqs, seq_len, state_in=None, mix_prev=None, state_prev=None):
    carry = state_in is None
    a_width = _HEADS * _A_HEAD
    kw = _HEADS * _DK
    if carry:
        rows = 256
        blocks_per_seq = seq_len // rows
        grid = (n_seqs, blocks_per_seq)
        first = row_start // rows
        rmap = lambda col: (lambda n, c: (first + n * blocks_per_seq + c, col))
        a_chunk, a_seq, spg = _A_CHUNK, _A_CHUNK, 1
        state_shape = (n_slots, n_seqs, _HEADS, _DK, _DV)
        state_spec = pl.BlockSpec((None, None, _HEADS, _DK, _DV),
                                  lambda n, c: (layer_slot, n, 0, 0, 0))
        sem = ("arbitrary", "arbitrary")
    else:
        rows = _GLA_ROWS
        spg = rows // seq_len
        grid = (n_seqs // spg,)
        first = row_start // rows
        rmap = lambda col: (lambda g: (first + g, col))
        a_chunk, a_seq = rows, seq_len
        state_shape = (n_slots, n_seqs, _HEADS, _DK, _DV)
        state_spec = pl.BlockSpec((None, spg, _HEADS, _DK, _DV),
                                  lambda g: (layer_slot, g, 0, 0, 0))
        sem = ("arbitrary",)
    const = (lambda *_: (0, 0, 0))
    in_specs = [
        pl.BlockSpec((rows, a_width), rmap(0)),
        pl.BlockSpec((rows, a_width), rmap(1)),
        pl.BlockSpec((rows, kw), rmap(2 * a_width // kw)),
        pl.BlockSpec((rows, kw), rmap(2 * a_width // kw + 1)),
        pl.BlockSpec((rows, a_width), rmap(3)),
        pl.BlockSpec((rows, a_width), rmap(4)),
        pl.BlockSpec((rows, kw), rmap(0)),
        pl.BlockSpec(wmix.shape, const),
        pl.BlockSpec(bcol.shape, const),
        pl.BlockSpec(gout.shape, lambda *_: (0, 0)),
    ]
    args = [proj, proj, proj, proj, proj, proj, loga, wmix, bcol, gout]
    if not carry:
        in_specs.append(state_spec)
        args.append(state_in)
    aliases = {}
    if mix_prev is not None:
        aliases[len(args)] = 0
        in_specs.append(pl.BlockSpec(memory_space=pl.ANY))
        args.append(mix_prev)
    if state_prev is not None:
        aliases[len(args)] = 1
        in_specs.append(pl.BlockSpec(memory_space=pl.ANY))
        args.append(state_prev)
    return pl.pallas_call(
        functools.partial(_mixer_even_kernel, rows=rows, a_chunk=a_chunk, a_seq=a_seq,
                          seqs_per_group=spg, carry_state=carry),
        grid=grid,
        in_specs=in_specs,
        out_specs=[pl.BlockSpec((rows, 2 * a_width), rmap(0)), state_spec],
        out_shape=[jax.ShapeDtypeStruct((total_rows, 2 * a_width), _BF16),
                   jax.ShapeDtypeStruct(state_shape, _F32)],
        input_output_aliases=aliases,
        compiler_params=_params(sem),
        name="mixer_even_long" if carry else "mixer_even_short",
    )(*args)


def _conv_outproj_kernel(z_ref, halo_ref, bg_ref, cw_ref, hist_ref, w_ref, x_ref, o_ref,
                         zs_ref, h1_ref, h2_ref, *, rows, rows_long, seq_long, seq_short):
    i = pl.program_id(0)
    cols = z_ref.shape[1]
    keep = _CONV_TAPS - 1

    @pl.when(i == 0)
    def _():
        h1_ref[...] = jnp.zeros(h1_ref.shape, _F32)
        h2_ref[...] = jnp.zeros(h2_ref.shape, _F32)

    @pl.when((i + 1) * rows > rows_long)
    def _():
        nh = hist_ref.shape[0]
        r = i * rows + lax.broadcasted_iota(jnp.int32, (rows, nh), 0) - rows_long
        c = lax.broadcasted_iota(jnp.int32, (rows, nh), 1)
        rr = jnp.maximum(r, 0)
        n, t = rr // seq_short, rr % seq_short
        pieces = _split3(hist_ref[...])
        for back, h_ref in ((1, h1_ref), (2, h2_ref)):
            pick = (r >= 0) & (t < back) & (c == keep * n + keep - back + t)
            sel = jnp.where(pick, 1.0, 0.0).astype(_BF16)
            h_ref[...] = _dot(sel, pieces[0]) + _dot(sel, pieces[1]) + _dot(sel, pieces[2])

    z = z_ref[...]
    zs_ref[8:8 + rows, :] = z
    zs_ref[0:8, :] = halo_ref[...]
    prev1 = zs_ref[pl.ds(7, rows), :]
    prev2 = zs_ref[pl.ds(6, rows), :]
    row = i * rows + lax.broadcasted_iota(jnp.int32, (rows, cols), 0)
    t = jnp.where(row >= rows_long, (row - rows_long) % seq_short, row % seq_long)
    prev1 = jnp.where(t >= 1, prev1, h1_ref[...])
    prev2 = jnp.where(t >= 2, prev2, h2_ref[...])
    conv = prev2 * cw_ref[0:1, :] + prev1 * cw_ref[1:2, :] + z * cw_ref[2:3, :]
    gated = (bg_ref[...].astype(_F32) * conv).astype(_BF16)
    o_ref[...] = x_ref[...] + _dot(gated, w_ref[...])


def _conv_outproj_residual(z, bg, cw, hist, w, layer, x, *, rows_long, seq_long, seq_short):
    m, c = z.shape
    n = w.shape[2]
    rows = _ROW_BLOCK // 4
    assert m % rows == 0 and rows % 8 == 0 and _CONV_TAPS == 3
    row_block = lambda width: pl.BlockSpec((rows, width), lambda i: (i, 0))
    return pl.pallas_call(
        functools.partial(_conv_outproj_kernel, rows=rows, rows_long=rows_long,
                          seq_long=seq_long, seq_short=seq_short),
        grid=(m // rows,),
        in_specs=[
            row_block(c),
            pl.BlockSpec((8, c), lambda i: (jnp.maximum(i * (rows // 8) - 1, 0), 0)),
            row_block(c),
            pl.BlockSpec((_CONV_TAPS, c), lambda i: (0, 0)),
            pl.BlockSpec(hist.shape, lambda i: (0, 0)),
            pl.BlockSpec((None, c, n), lambda i: (layer, 0, 0), pipeline_mode=pl.Buffered(1)),
            row_block(n),
        ],
        out_specs=row_block(n),
        out_shape=jax.ShapeDtypeStruct((m, n), _F32),
        scratch_shapes=[pltpu.VMEM((rows + 8, c), _F32),
                        pltpu.VMEM((rows, c), _F32),
                        pltpu.VMEM((rows, c), _F32)],
        compiler_params=_params(("arbitrary",)),
        name="conv_outproj_residual",
    )(z, z, bg, cw, hist, w, x)


def kernel(x_prompt, x_sample, state_gla, state_conv, norm_mix, norm_ffn, norm_final, w_in_even, w_gate_up, b_gate, w_spatial, b_spatial, g_gla_out, w_out_even, w_in_odd, conv_w, w_out_odd, w_ffn_up, w_ffn_down):
    n_p, t_p, d = x_prompt.shape
    n_s, t_s, _ = x_sample.shape
    rows_p, rows_s = n_p * t_p, n_s * t_s
    total = rows_p + rows_s
    depth = norm_mix.shape[0]
    n_even = w_in_even.shape[0]
    a_width = _HEADS * _A_HEAD
    main_cols = w_in_even.shape[2] - _GATE_RANK

    x = jnp.concatenate([x_prompt.reshape(rows_p, d), x_sample.reshape(rows_s, d)], axis=0)

    reps = _GLA_ROWS // t_s
    wmix_s = jnp.tile(w_spatial[:, :, :t_s, :t_s], (1, 1, reps, reps))
    bcol_s = jnp.tile(b_spatial[:, :, :t_s], (1, 1, reps))[..., None]
    bcol_p = b_spatial[..., None]

    w_in_even_b = w_in_even.astype(_BF16)
    w_in_odd_b = w_in_odd.astype(_BF16)
    w_out_even_b = w_out_even.astype(_BF16)
    w_out_odd_b = w_out_odd.astype(_BF16)

    v_rows, conv_p, conv_s = [], [], []
    gla_p = gla_s = None
    for l in range(depth):
        i = l // 2
        g_mix = norm_mix[l][None, :]
        if l % 2 == 0:
            w_glr = jnp.pad(w_in_even_b[i, :, main_cols:], ((0, 0), (0, 128 - _GATE_RANK)))
            w_gate = jnp.pad(w_gate_up[i], ((0, 128 - _GATE_RANK), (0, 0))).astype(_BF16)
            proj, v32, loga = _inproj_even(x, g_mix, w_in_even_b, i, w_glr, w_gate, b_gate[i][None, :],
                                           tail_rows=rows_s)
            gout = g_gla_out[i][None, :]
            mix, gla_p = _mixer_even(
                proj, loga, w_spatial[i], bcol_p[i], gout, layer_slot=i, n_slots=n_even,
                total_rows=total, row_start=0, n_seqs=n_p, seq_len=t_p, state_prev=gla_p)
            mix, gla_s = _mixer_even(
                proj, loga, wmix_s[i], bcol_s[i], gout, layer_slot=i, n_slots=n_even,
                total_rows=total, row_start=rows_p, n_seqs=n_s, seq_len=t_s,
                state_in=state_gla, mix_prev=mix, state_prev=gla_s)
            v_rows.append(v32.reshape(n_s, t_s, a_width))
            x = _outproj_residual(mix, w_out_even_b, i, x)
        else:
            bg, z = _inproj_odd(x, g_mix, w_in_odd_b, i)
            keep = _CONV_TAPS - 1
            conv_p.append(jnp.stack([z[(n + 1) * t_p - keep:(n + 1) * t_p] for n in range(n_p)]))
            conv_s.append(z[rows_p:].reshape(n_s, t_s, d)[:, t_s - keep:])
            hist = state_conv[i].astype(_F32).reshape(n_s * keep, d)
            x = _conv_outproj_residual(z, bg, conv_w[i], hist, w_out_odd_b, i, x,
                                       rows_long=rows_p, seq_long=t_p, seq_short=t_s)
        g_ffn = norm_ffn[l][None, :]
        if l < depth - 1:
            x = _ffn_residual(x, g_ffn, w_ffn_up, w_ffn_down, l)
        else:
            y_p, y_s = _ffn_residual(x, g_ffn, w_ffn_up, w_ffn_down, l,
                                     final_gain=norm_final[None, :], tail_rows=rows_s)

    return (y_p.reshape(n_p, t_p, d), y_s.reshape(n_s, t_s, d), gla_p, gla_s,
            jnp.stack(conv_p), jnp.stack(conv_s), jnp.stack(v_rows))
```

```python
import functools

import jax
import jax.numpy as jnp
from jax import lax
from jax.experimental import pallas as pl
from jax.experimental.pallas import tpu as pltpu

_F32 = jnp.float32
_BF16 = jnp.bfloat16

_EPS = 1e-6
_HEADS = 4
_A_HEAD = 256
_A_CHUNK = 128
_DK = 128
_DV = 256
_GATE_RANK = 16
_GATE_TAU = 16.0
_GLA_ROWS = 64
_CONV_TAPS = 3

_ROW_BLOCK = 1088
_COL_TILE = 512
_MIB = 1024 * 1024
_VMEM_LIMIT = 56 * _MIB
_VMEM_LIMIT_FFN = 62 * _MIB


def _params(semantics, vmem_limit=_VMEM_LIMIT):
    return pltpu.CompilerParams(dimension_semantics=semantics, vmem_limit_bytes=vmem_limit)


def _layer_spec(stacked, layer):
    tail = (0,) * (stacked.ndim - 1)
    return pl.BlockSpec((None,) + stacked.shape[1:], lambda *_: (layer,) + tail)


def _rmsnorm_rows(x, g):
    ms = jnp.mean(x * x, axis=-1, keepdims=True)
    return (x * lax.rsqrt(ms + _EPS)) * g


def _gelu_tanh(x):
    return x * (0.5 * (1.0 + jnp.tanh(0.7978845608028654 * (x + 0.044715 * (x * x * x)))))


def _log_sigmoid(x):
    return jnp.minimum(x, 0.0) - jnp.log(1.0 + jnp.exp(-jnp.abs(x)))


def _dot(a, b):
    return jnp.dot(a, b, preferred_element_type=_F32)


def _dot_nt(a, b):
    return lax.dot_general(a, b, (((1,), (1,)), ((), ())), preferred_element_type=_F32)


def _dot_tn(a, b):
    return lax.dot_general(a, b, (((0,), (0,)), ((), ())), preferred_element_type=_F32)


def _inproj_even_kernel(x_ref, g_ref, w_ref, wglr_ref, wgate_ref, bgate_ref,
                        proj_ref, v32_ref, loga_ref, hn_ref, *, gelu_tiles, v_tiles, tail_start):
    i, j = pl.program_id(0), pl.program_id(1)

    def project(hn, activate):
        acc = _dot(hn, w_ref[...])
        proj_ref[...] = (_gelu_tanh(acc) if activate else acc).astype(_BF16)

    @pl.when(j == 0)
    def _():
        hn = _rmsnorm_rows(x_ref[...], g_ref[...]).astype(_BF16)
        hn_ref[...] = hn
        glr = _dot(hn, wglr_ref[...])
        gate = _dot(glr.astype(_BF16), wgate_ref[...]) + bgate_ref[...]
        loga_ref[...] = _log_sigmoid(gate) * (1.0 / _GATE_TAU)
        project(hn, True)

    @pl.when((j > 0) & (j < gelu_tiles))
    def _():
        project(hn_ref[...], True)

    @pl.when(j >= gelu_tiles)
    def _():
        project(hn_ref[...], False)

    @pl.when((i == pl.num_programs(0) - 1) & (j >= v_tiles[0]) & (j < v_tiles[1]))
    def _():
        v32_ref[...] = _gelu_tanh(_dot(hn_ref[tail_start:, :], w_ref[...]))


def _inproj_even(x, g, g_row, w_all, layer, w_glr, w_gate, b_gate, tail_rows):
    m, d = x.shape
    n = w_all.shape[2] - _GATE_RANK
    bm, tn = _ROW_BLOCK, 2 * _COL_TILE
    assert tail_rows <= bm and (bm - tail_rows) % 16 == 0
    a_width = _HEADS * _A_HEAD
    gelu_tiles = 2 * a_width // tn
    v_tiles = (a_width // tn, 2 * a_width // tn)
    nk = w_gate.shape[2]

    def v32_map(i, j):
        tile = jnp.clip(j - v_tiles[0], 0, v_tiles[1] - v_tiles[0] - 1)
        return (0, jnp.where(i == m // bm - 1, tile, 0))

    return pl.pallas_call(
        functools.partial(_inproj_even_kernel, gelu_tiles=gelu_tiles, v_tiles=v_tiles,
                          tail_start=bm - tail_rows),
        grid=(m // bm, n // tn),
        in_specs=[
            pl.BlockSpec((bm, d), lambda i, j: (i, 0)),
            _layer_spec(g, g_row),
            pl.BlockSpec((None, d, tn), lambda i, j: (layer, 0, j)),
            _layer_spec(w_glr, layer),
            _layer_spec(w_gate, layer),
            _layer_spec(b_gate, layer),
        ],
        out_specs=[
            pl.BlockSpec((bm, tn), lambda i, j: (i, j)),
            pl.BlockSpec((tail_rows, tn), v32_map),
            pl.BlockSpec((bm, nk), lambda i, j: (i, 0)),
        ],
        out_shape=[
            jax.ShapeDtypeStruct((m, n), _BF16),
            jax.ShapeDtypeStruct((tail_rows, a_width), _F32),
            jax.ShapeDtypeStruct((m, nk), _F32),
        ],
        scratch_shapes=[pltpu.VMEM((bm, d), _BF16)],
        compiler_params=_params(("arbitrary", "arbitrary")),
        name="inproj_even",
    )(x, g, w_all, w_glr, w_gate, b_gate)


def _inproj_odd_kernel(x_ref, g_ref, wb_ref, wc_ref, wh_ref, bg_ref, z_ref, hn_ref):
    j = pl.program_id(1)

    def project(hn):
        bg_ref[...] = _dot(hn, wb_ref[...]).astype(_BF16)
        z_ref[...] = _dot(hn, wc_ref[...]) * _dot(hn, wh_ref[...])

    @pl.when(j == 0)
    def _():
        hn = _rmsnorm_rows(x_ref[...], g_ref[...]).astype(_BF16)
        hn_ref[...] = hn
        project(hn)

    @pl.when(j > 0)
    def _():
        project(hn_ref[...])


def _inproj_odd(x, g, g_row, w, layer):
    m, d = x.shape
    c = w.shape[2] // 3
    bm, tn = _ROW_BLOCK, _COL_TILE
    nt = c // tn
    return pl.pallas_call(
        _inproj_odd_kernel,
        grid=(m // bm, nt),
        in_specs=[
            pl.BlockSpec((bm, d), lambda i, j: (i, 0)),
            _layer_spec(g, g_row),
            pl.BlockSpec((None, d, tn), lambda i, j: (layer, 0, j)),
            pl.BlockSpec((None, d, tn), lambda i, j: (layer, 0, j + nt)),
            pl.BlockSpec((None, d, tn), lambda i, j: (layer, 0, j + 2 * nt)),
        ],
        out_specs=[
            pl.BlockSpec((bm, tn), lambda i, j: (i, j)),
            pl.BlockSpec((bm, tn), lambda i, j: (i, j)),
        ],
        out_shape=[
            jax.ShapeDtypeStruct((m, c), _BF16),
            jax.ShapeDtypeStruct((m, c), _F32),
        ],
        scratch_shapes=[pltpu.VMEM((bm, d), _BF16)],
        compiler_params=_params(("arbitrary", "arbitrary")),
        name="inproj_odd",
    )(x, g, w, w, w)


def _outproj_kernel(a_ref, w_ref, x_ref, o_ref, wb_ref):
    @pl.when(pl.program_id(0) == 0)
    def _():
        wb_ref[...] = w_ref[...].astype(_BF16)

    o_ref[...] = x_ref[...] + _dot(a_ref[...], wb_ref[...])


def _resident_weight_spec(w, layer):
    return pl.BlockSpec((None,) + w.shape[1:], lambda i: (layer, 0, 0), pipeline_mode=pl.Buffered(1))


def _outproj_residual(a, w, layer, x):
    m, k = a.shape
    n = w.shape[2]
    bm = _ROW_BLOCK // 2
    return pl.pallas_call(
        _outproj_kernel,
        grid=(m // bm,),
        in_specs=[
            pl.BlockSpec((bm, k), lambda i: (i, 0)),
            _resident_weight_spec(w, layer),
            pl.BlockSpec((bm, n), lambda i: (i, 0)),
        ],
        out_specs=pl.BlockSpec((bm, n), lambda i: (i, 0)),
        out_shape=jax.ShapeDtypeStruct((m, n), _F32),
        scratch_shapes=[pltpu.VMEM((k, n), _BF16)],
        compiler_params=_params(("arbitrary",)),
        name="outproj_residual",
    )(a, w, x)


def _ffn_kernel(x_ref, g_ref, wup_ref, wdn_ref, *rest, tail_start):
    final = len(rest) == 4
    if final:
        gfin_ref, o_ref, tail_ref, hn_ref = rest
    else:
        o_ref, hn_ref = rest
    i, j = pl.program_id(0), pl.program_id(1)

    def mlp_tile(hn):
        h = _dot(hn, wup_ref[...].astype(_BF16))
        h = jnp.square(jnp.maximum(h, 0.0)).astype(_BF16)
        return _dot(h, wdn_ref[...].astype(_BF16))

    @pl.when(j == 0)
    def _():
        x = x_ref[...]
        hn = _rmsnorm_rows(x, g_ref[...]).astype(_BF16)
        hn_ref[...] = hn
        o_ref[...] = x + mlp_tile(hn)

    @pl.when(j > 0)
    def _():
        o_ref[...] += mlp_tile(hn_ref[...])

    if final:
        @pl.when(j == pl.num_programs(1) - 1)
        def _():
            o_ref[...] = _rmsnorm_rows(o_ref[...], gfin_ref[...])

            @pl.when(i == pl.num_programs(0) - 1)
            def _():
                tail_ref[...] = o_ref[tail_start:, :]


def _ffn_residual(x, g, w_up, w_down, layer, final_gain=None, tail_rows=0):
    m, d = x.shape
    f = w_up.shape[2]
    bm, tf = _ROW_BLOCK, _COL_TILE
    in_specs = [
        pl.BlockSpec((bm, d), lambda i, j: (i, 0)),
        _layer_spec(g, layer),
        pl.BlockSpec((None, d, tf), lambda i, j: (layer, 0, j)),
        pl.BlockSpec((None, tf, d), lambda i, j: (layer, j, 0)),
    ]
    args = [x, g, w_up, w_down]
    row_block = pl.BlockSpec((bm, d), lambda i, j: (i, 0))
    if final_gain is None:
        out_specs = row_block
        out_shape = jax.ShapeDtypeStruct((m, d), _F32)
    else:
        assert tail_rows <= bm and (bm - tail_rows) % 8 == 0
        in_specs.append(_layer_spec(final_gain, 0))
        args.append(final_gain)
        out_specs = [row_block, pl.BlockSpec((tail_rows, d), lambda i, j: (0, 0))]
        out_shape = [jax.ShapeDtypeStruct((m - tail_rows, d), _F32),
                     jax.ShapeDtypeStruct((tail_rows, d), _F32)]
    return pl.pallas_call(
        functools.partial(_ffn_kernel, tail_start=bm - tail_rows),
        grid=(m // bm, f // tf),
        in_specs=in_specs,
        out_specs=out_specs,
        out_shape=out_shape,
        scratch_shapes=[pltpu.VMEM((bm, d), _BF16)],
        compiler_params=_params(("arbitrary", "arbitrary"), _VMEM_LIMIT_FFN),
        name="ffn_residual" if final_gain is None else "ffn_final",
    )(*args)


def _split3(x):
    hi = x.astype(_BF16)
    r1 = x - hi.astype(_F32)
    mid = r1.astype(_BF16)
    lo = (r1 - mid.astype(_F32)).astype(_BF16)
    return hi, mid, lo


def _mixer_even_kernel(*refs, rows, a_chunk, a_seq, seqs_per_group, carry_state):
    (u_ref, v_ref, q_ref, k_ref, vb_ref, r_ref, la_ref, wmix_ref, bcol_ref, gout_ref) = refs[:10]
    rest = refs[10:]
    if carry_state:
        mix_ref, sout_ref = rest[-2:]
        sin_ref = None
    else:
        sin_ref = rest[0]
        mix_ref, sout_ref = rest[-2:]

    tt = lax.broadcasted_iota(jnp.int32, (a_chunk, a_chunk), 0)
    ss = lax.broadcasted_iota(jnp.int32, (a_chunk, a_chunk), 1)
    a_mask = (ss <= tt) & ((tt // a_seq) == (ss // a_seq))
    for h in range(_HEADS):
        cols = slice(h * _A_HEAD, (h + 1) * _A_HEAD)
        wm = jnp.where(a_mask, wmix_ref[h], 0.0).astype(_BF16)
        bias = bcol_ref[h]
        for c in range(rows // a_chunk):
            rws = slice(c * a_chunk, (c + 1) * a_chunk)
            mixed = _dot(wm, v_ref[rws, cols]) + bias
            mix_ref[rws, cols] = (u_ref[rws, cols].astype(_F32) * mixed).astype(_BF16)

    gr = _GLA_ROWS
    seq_rows = gr // seqs_per_group
    ti = lax.broadcasted_iota(jnp.int32, (gr, gr), 0)
    si = lax.broadcasted_iota(jnp.int32, (gr, gr), 1)
    same_seq = (ti // seq_rows) == (si // seq_rows)
    causal = (si <= ti) & same_seq
    cum_and_total = jnp.concatenate(
        [jnp.where(causal, 1.0, 0.0), jnp.where(same_seq, 1.0, 0.0)], axis=0).astype(_BF16)
    kw = _HEADS * _DK
    row_k = lax.broadcasted_iota(jnp.int32, (gr, _DK), 0)
    row_v = lax.broadcasted_iota(jnp.int32, (gr, _DV), 0)
    pos_all = lax.broadcasted_iota(jnp.int32, (gr, kw), 0) % seq_rows
    ones_v = jnp.ones((gr, _DV), _BF16)
    scale = _DK ** -0.5
    a_width = _HEADS * _A_HEAD

    if carry_state:
        @pl.when(pl.program_id(1) == 0)
        def _():
            sout_ref[...] = jnp.zeros(sout_ref.shape, _F32)
        states = [sout_ref[h] for h in range(_HEADS)]

    for gi in range(rows // gr):
        rws = slice(gi * gr, (gi + 1) * gr)
        la = la_ref[rws, :]
        la_hi = la.astype(_BF16)
        la_lo = (la - la_hi.astype(_F32)).astype(_BF16)
        sums = _dot(cum_and_total, jnp.concatenate([la_hi, la_lo], axis=1))
        g_cum = sums[:gr, :kw] + sums[:gr, kw:]
        g_tot = sums[gr:, :kw] + sums[gr:, kw:]
        q = q_ref[rws, :].astype(_F32) * scale
        k = k_ref[rws, :].astype(_F32)
        q_in_all = (q * jnp.exp(g_cum)).astype(_BF16)
        k_in_all = (k * jnp.exp(-g_cum)).astype(_BF16)
        k_dec_all = (k * jnp.exp(g_tot - g_cum)).astype(_BF16)
        t_hi, t_mid, t_lo = _split3(g_tot)
        pieces_all = jnp.where(pos_all == 0, t_hi,
                               jnp.where(pos_all == 1, t_mid,
                                         jnp.where(pos_all == 2, t_lo, jnp.zeros_like(t_hi))))
        for h in range(_HEADS):
            kc = slice(h * _DK, (h + 1) * _DK)
            vc = slice(h * _DV, (h + 1) * _DV)
            q_in, k_in, k_dec, pieces = q_in_all[:, kc], k_in_all[:, kc], k_dec_all[:, kc], pieces_all[:, kc]
            v = vb_ref[rws, vc]
            scores = jnp.where(causal, _dot_nt(q_in, k_in), 0.0).astype(_BF16)
            if carry_state:
                o = _dot(jnp.concatenate([q_in, scores], axis=1),
                         jnp.concatenate([states[h].astype(_BF16), v], axis=0))
                decay = jnp.exp(_dot_tn(pieces, ones_v))
                states[h] = states[h] * decay + _dot_tn(k_dec, v)
            else:
                o = _dot(scores, v)
                zero = jnp.zeros_like(pieces)
                for s in range(seqs_per_group):
                    in_seq_k = (row_k // seq_rows) == s
                    in_seq_v = (row_v // seq_rows) == s
                    s0 = sin_ref[s, h]
                    o = o + jnp.where(in_seq_v, _dot(q_in, s0.astype(_BF16)), 0.0)
                    decay = jnp.exp(_dot_tn(jnp.where(in_seq_k, pieces, zero), ones_v))
                    kd = jnp.where(in_seq_k, k_dec, zero)
                    sout_ref[s, h] = s0 * decay + _dot_tn(kd, v)
            ms = jnp.mean(o * o, axis=-1, keepdims=True)
            on = (o * lax.rsqrt(ms + _EPS)) * gout_ref[:, vc]
            r = r_ref[rws, vc].astype(_F32)
            gated = on * (r * (1.0 / (1.0 + jnp.exp(-r))))
            mix_ref[rws, a_width + h * _DV:a_width + (h + 1) * _DV] = gated.astype(_BF16)

    if carry_state:
        for h in range(_HEADS):
            sout_ref[h] = states[h]


def _mixer_even(proj, loga, wmix, bcol, gout, *, layer_slot, n_slots, total_rows,
                row_start, n_seqs, seq_len, state_in=None, mix_prev=None, state_prev=None):
    carry = state_in is None
    a_width = _HEADS * _A_HEAD
    kw = _HEADS * _DK
    if carry:
        rows = 512
        blocks_per_seq = seq_len // rows
        grid = (n_seqs, blocks_per_seq)
        first = row_start // rows
        rmap = lambda col: (lambda n, c: (first + n * blocks_per_seq + c, col))
        a_chunk, a_seq, spg = _A_CHUNK, _A_CHUNK, 1
        state_shape = (n_slots, n_seqs, _HEADS, _DK, _DV)
        state_spec = pl.BlockSpec((None, None, _HEADS, _DK, _DV),
                                  lambda n, c: (layer_slot, n, 0, 0, 0))
        sem = ("arbitrary", "arbitrary")
    else:
        rows = _GLA_ROWS
        spg = rows // seq_len
        grid = (n_seqs // spg,)
        first = row_start // rows
        rmap = lambda col: (lambda g: (first + g, col))
        a_chunk, a_seq = rows, seq_len
        state_shape = (n_slots, n_seqs, _HEADS, _DK, _DV)
        state_spec = pl.BlockSpec((None, spg, _HEADS, _DK, _DV),
                                  lambda g: (layer_slot, g, 0, 0, 0))
        sem = ("arbitrary",)
    in_specs = [
        pl.BlockSpec((rows, a_width), rmap(0)),
        pl.BlockSpec((rows, a_width), rmap(1)),
        pl.BlockSpec((rows, kw), rmap(2 * a_width // kw)),
        pl.BlockSpec((rows, kw), rmap(2 * a_width // kw + 1)),
        pl.BlockSpec((rows, a_width), rmap(3)),
        pl.BlockSpec((rows, a_width), rmap(4)),
        pl.BlockSpec((rows, kw), rmap(0)),
        _layer_spec(wmix, layer_slot),
        _layer_spec(bcol, layer_slot),
        _layer_spec(gout, layer_slot),
    ]
    args = [proj, proj, proj, proj, proj, proj, loga, wmix, bcol, gout]
    if not carry:
        in_specs.append(state_spec)
        args.append(state_in)
    aliases = {}
    if mix_prev is not None:
        aliases[len(args)] = 0
        in_specs.append(pl.BlockSpec(memory_space=pl.ANY))
        args.append(mix_prev)
    if state_prev is not None:
        aliases[len(args)] = 1
        in_specs.append(pl.BlockSpec(memory_space=pl.ANY))
        args.append(state_prev)
    return pl.pallas_call(
        functools.partial(_mixer_even_kernel, rows=rows, a_chunk=a_chunk, a_seq=a_seq,
                          seqs_per_group=spg, carry_state=carry),
        grid=grid,
        in_specs=in_specs,
        out_specs=[pl.BlockSpec((rows, 2 * a_width), rmap(0)), state_spec],
        out_shape=[jax.ShapeDtypeStruct((total_rows, 2 * a_width), _BF16),
                   jax.ShapeDtypeStruct(state_shape, _F32)],
        input_output_aliases=aliases,
        compiler_params=_params(sem),
        name="mixer_even_long" if carry else "mixer_even_short",
    )(*args)


def _conv_outproj_kernel(z_ref, halo_ref, bg_ref, cw_ref, hist_ref, w_ref, x_ref, o_ref,
                         zs_ref, h1_ref, h2_ref, wb_ref,
                         *, rows, rows_long, seq_long, seq_short):
    i = pl.program_id(0)
    cols = z_ref.shape[1]
    keep = _CONV_TAPS - 1

    @pl.when(i == 0)
    def _():
        wb_ref[...] = w_ref[...].astype(_BF16)
        h1_ref[...] = jnp.zeros(h1_ref.shape, _F32)
        h2_ref[...] = jnp.zeros(h2_ref.shape, _F32)

    @pl.when((i + 1) * rows > rows_long)
    def _():
        nh = hist_ref.shape[0]
        r = i * rows + lax.broadcasted_iota(jnp.int32, (rows, nh), 0) - rows_long
        c = lax.broadcasted_iota(jnp.int32, (rows, nh), 1)
        rr = jnp.maximum(r, 0)
        n, t = rr // seq_short, rr % seq_short
        pieces = _split3(hist_ref[...])
        for back, h_ref in ((1, h1_ref), (2, h2_ref)):
            pick = (r >= 0) & (t < back) & (c == keep * n + keep - back + t)
            sel = jnp.where(pick, 1.0, 0.0).astype(_BF16)
            h_ref[...] = _dot(sel, pieces[0]) + _dot(sel, pieces[1]) + _dot(sel, pieces[2])

    zs_ref[8:8 + rows, :] = z_ref[...]
    zs_ref[0:8, :] = halo_ref[...]
    lanes = 128
    row = i * rows + lax.broadcasted_iota(jnp.int32, (rows, lanes), 0)
    t = jnp.where(row >= rows_long, (row - rows_long) % seq_short, row % seq_long)
    has_prev1, has_prev2 = t >= 1, t >= 2

    def gated(sl):
        prev1 = jnp.where(has_prev1, zs_ref[pl.ds(7, rows), sl], h1_ref[:, sl])
        prev2 = jnp.where(has_prev2, zs_ref[pl.ds(6, rows), sl], h2_ref[:, sl])
        conv = prev2 * cw_ref[0:1, sl] + prev1 * cw_ref[1:2, sl] + z_ref[:, sl] * cw_ref[2:3, sl]
        return (bg_ref[:, sl].astype(_F32) * conv).astype(_BF16)

    chunk = 2 * lanes
    acc = x_ref[...]
    for kb in range(cols // chunk):
        lo = kb * chunk
        g = jnp.concatenate([gated(slice(lo, lo + lanes)), gated(slice(lo + lanes, lo + chunk))], axis=1)
        acc = acc + _dot(g, wb_ref[lo:lo + chunk, :])
    o_ref[...] = acc


def _conv_outproj_residual(z, bg, cw, hist, w, layer, x, *, rows_long, seq_long, seq_short):
    m, c = z.shape
    n = w.shape[2]
    rows = _ROW_BLOCK // 4
    assert m % rows == 0 and rows % 8 == 0 and _CONV_TAPS == 3
    row_block = lambda width: pl.BlockSpec((rows, width), lambda i: (i, 0))
    return pl.pallas_call(
        functools.partial(_conv_outproj_kernel, rows=rows, rows_long=rows_long,
                          seq_long=seq_long, seq_short=seq_short),
        grid=(m // rows,),
        in_specs=[
            row_block(c),
            pl.BlockSpec((8, c), lambda i: (jnp.maximum(i * (rows // 8) - 1, 0), 0)),
            row_block(c),
            _layer_spec(cw, layer),
            _layer_spec(hist, layer),
            _resident_weight_spec(w, layer),
            row_block(n),
        ],
        out_specs=row_block(n),
        out_shape=jax.ShapeDtypeStruct((m, n), _F32),
        scratch_shapes=[pltpu.VMEM((rows + 8, c), _F32),
                        pltpu.VMEM((rows, c), _F32),
                        pltpu.VMEM((rows, c), _F32),
                        pltpu.VMEM((c, n), _BF16)],
        compiler_params=_params(("arbitrary",)),
        name="conv_outproj_residual",
    )(z, z, bg, cw, hist, w, x)


def kernel(x_prompt, x_sample, state_gla, state_conv, norm_mix, norm_ffn, norm_final, w_in_even, w_gate_up, b_gate, w_spatial, b_spatial, g_gla_out, w_out_even, w_in_odd, conv_w, w_out_odd, w_ffn_up, w_ffn_down):
    n_p, t_p, d = x_prompt.shape
    n_s, t_s, _ = x_sample.shape
    rows_p, rows_s = n_p * t_p, n_s * t_s
    total = rows_p + rows_s
    depth = norm_mix.shape[0]
    n_even = w_in_even.shape[0]
    a_width = _HEADS * _A_HEAD
    main_cols = w_in_even.shape[2] - _GATE_RANK

    x = jnp.concatenate([x_prompt.reshape(rows_p, d), x_sample.reshape(rows_s, d)], axis=0)

    reps = _GLA_ROWS // t_s
    wmix_s = jnp.tile(w_spatial[:, :, :t_s, :t_s], (1, 1, reps, reps))
    bcol_s = jnp.tile(b_spatial[:, :, :t_s], (1, 1, reps))[..., None]
    bcol_p = b_spatial[..., None]
    gain_mix, gain_ffn = norm_mix[:, None, :], norm_ffn[:, None, :]
    gain_final = norm_final[None, None, :]
    gain_gla = g_gla_out[:, None, :]
    bias_gate = b_gate[:, None, :]
    keep = _CONV_TAPS - 1
    hist = state_conv.astype(_F32).reshape(state_conv.shape[0], n_s * keep, d)

    w_in_even_b = w_in_even.astype(_BF16)
    w_in_odd_b = w_in_odd.astype(_BF16)
    w_glr = jnp.pad(w_in_even_b[:, :, main_cols:], ((0, 0), (0, 0), (0, 128 - _GATE_RANK)))
    w_gate = jnp.pad(w_gate_up, ((0, 0), (0, 128 - _GATE_RANK), (0, 0))).astype(_BF16)

    v_rows, conv_p, conv_s = [], [], []
    gla_p = gla_s = None
    for l in range(depth):
        i = l // 2
        if l % 2 == 0:
            proj, v32, loga = _inproj_even(x, gain_mix, l, w_in_even_b, i, w_glr, w_gate, bias_gate,
                                           tail_rows=rows_s)
            mix, gla_p = _mixer_even(
                proj, loga, w_spatial, bcol_p, gain_gla, layer_slot=i, n_slots=n_even,
                total_rows=total, row_start=0, n_seqs=n_p, seq_len=t_p, state_prev=gla_p)
            mix, gla_s = _mixer_even(
                proj, loga, wmix_s, bcol_s, gain_gla, layer_slot=i, n_slots=n_even,
                total_rows=total, row_start=rows_p, n_seqs=n_s, seq_len=t_s,
                state_in=state_gla, mix_prev=mix, state_prev=gla_s)
            v_rows.append(v32.reshape(n_s, t_s, a_width))
            x = _outproj_residual(mix, w_out_even, i, x)
        else:
            bg, z = _inproj_odd(x, gain_mix, l, w_in_odd_b, i)
            conv_p.append(jnp.stack([z[(n + 1) * t_p - keep:(n + 1) * t_p] for n in range(n_p)]))
            conv_s.append(z[rows_p:].reshape(n_s, t_s, d)[:, t_s - keep:])
            x = _conv_outproj_residual(z, bg, conv_w, hist, w_out_odd, i, x,
                                       rows_long=rows_p, seq_long=t_p, seq_short=t_s)
        if l < depth - 1:
            x = _ffn_residual(x, gain_ffn, w_ffn_up, w_ffn_down, l)
        else:
            y_p, y_s = _ffn_residual(x, gain_ffn, w_ffn_up, w_ffn_down, l,
                                     final_gain=gain_final, tail_rows=rows_s)

    return (y_p.reshape(n_p, t_p, d), y_s.reshape(n_s, t_s, d), gla_p, gla_s,
            jnp.stack(conv_p), jnp.stack(conv_s), jnp.stack(v_rows))
```

```python
import functools

import jax
import jax.numpy as jnp
from jax import lax
from jax.experimental import pallas as pl
from jax.experimental.pallas import tpu as pltpu

_F32 = jnp.float32
_BF16 = jnp.bfloat16

_EPS = 1e-6
_HEADS = 4
_A_HEAD = 256
_A_CHUNK = 128
_DK = 128
_DV = 256
_GATE_RANK = 16
_GATE_TAU = 16.0
_GLA_ROWS = 64
_CONV_TAPS = 3

_ROW_BLOCK = 1088
_COL_TILE = 512
_MIB = 1024 * 1024
_VMEM_LIMIT = 56 * _MIB
_VMEM_LIMIT_FFN = 62 * _MIB


def _params(semantics, vmem_limit=_VMEM_LIMIT):
    return pltpu.CompilerParams(dimension_semantics=semantics, vmem_limit_bytes=vmem_limit)


def _layer_spec(stacked, layer):
    tail = (0,) * (stacked.ndim - 1)
    return pl.BlockSpec((None,) + stacked.shape[1:], lambda *_: (layer,) + tail)


def _rmsnorm_rows(x, g):
    ms = jnp.mean(x * x, axis=-1, keepdims=True)
    return (x * lax.rsqrt(ms + _EPS)) * g


def _gelu_tanh(x):
    return x * (0.5 * (1.0 + jnp.tanh(0.7978845608028654 * (x + 0.044715 * (x * x * x)))))


def _log_sigmoid(x):
    return jnp.minimum(x, 0.0) - jnp.log(1.0 + jnp.exp(-jnp.abs(x)))


def _dot(a, b):
    return jnp.dot(a, b, preferred_element_type=_F32)


def _dot_nt(a, b):
    return lax.dot_general(a, b, (((1,), (1,)), ((), ())), preferred_element_type=_F32)


def _dot_tn(a, b):
    return lax.dot_general(a, b, (((0,), (0,)), ((), ())), preferred_element_type=_F32)


def _inproj_even_kernel(x_ref, g_ref, w_ref, wglr_ref, wgate_ref, bgate_ref,
                        proj_ref, v32_ref, loga_ref, hn_ref, *, gelu_tiles, v_tiles, tail_start):
    i, j = pl.program_id(0), pl.program_id(1)

    def project(hn, activate):
        acc = _dot_nt(hn, w_ref[...])
        proj_ref[...] = (_gelu_tanh(acc) if activate else acc).astype(_BF16)

    @pl.when(j == 0)
    def _():
        hn = _rmsnorm_rows(x_ref[...], g_ref[...]).astype(_BF16)
        hn_ref[...] = hn
        project(hn, True)

    @pl.when((j > 0) & (j < gelu_tiles))
    def _():
        project(hn_ref[...], True)

    @pl.when(j == gelu_tiles)
    def _():
        hn = hn_ref[...]
        glr = _dot_nt(hn, wglr_ref[...])
        gate = _dot(glr.astype(_BF16), wgate_ref[...]) + bgate_ref[...]
        loga_ref[...] = _log_sigmoid(gate) * (1.0 / _GATE_TAU)
        project(hn, False)

    @pl.when(j > gelu_tiles)
    def _():
        project(hn_ref[...], False)

    @pl.when((i == pl.num_programs(0) - 1) & (j >= v_tiles[0]) & (j < v_tiles[1]))
    def _():
        v32_ref[...] = _gelu_tanh(_dot_nt(hn_ref[tail_start:, :], w_ref[...]))


def _inproj_even(x, g, g_row, w_t, layer, w_gate, b_gate, tail_rows):
    m, d = x.shape
    n = w_t.shape[1] - _GATE_RANK
    assert n % _GATE_RANK == 0
    bm, tn = _ROW_BLOCK, 2 * _COL_TILE
    assert tail_rows <= bm and (bm - tail_rows) % 16 == 0
    a_width = _HEADS * _A_HEAD
    gelu_tiles = 2 * a_width // tn
    v_tiles = (a_width // tn, 2 * a_width // tn)
    nk = w_gate.shape[2]

    def v32_map(i, j):
        tile = jnp.clip(j - v_tiles[0], 0, v_tiles[1] - v_tiles[0] - 1)
        return (0, jnp.where(i == m // bm - 1, tile, 0))

    return pl.pallas_call(
        functools.partial(_inproj_even_kernel, gelu_tiles=gelu_tiles, v_tiles=v_tiles,
                          tail_start=bm - tail_rows),
        grid=(m // bm, n // tn),
        in_specs=[
            pl.BlockSpec((bm, d), lambda i, j: (i, 0)),
            _layer_spec(g, g_row),
            pl.BlockSpec((None, tn, d), lambda i, j: (layer, j, 0)),
            pl.BlockSpec((None, _GATE_RANK, d), lambda i, j: (layer, n // _GATE_RANK, 0)),
            _layer_spec(w_gate, layer),
            _layer_spec(b_gate, layer),
        ],
        out_specs=[
            pl.BlockSpec((bm, tn), lambda i, j: (i, j)),
            pl.BlockSpec((tail_rows, tn), v32_map),
            pl.BlockSpec((bm, nk), lambda i, j: (i, 0)),
        ],
        out_shape=[
            jax.ShapeDtypeStruct((m, n), _BF16),
            jax.ShapeDtypeStruct((tail_rows, a_width), _F32),
            jax.ShapeDtypeStruct((m, nk), _F32),
        ],
        scratch_shapes=[pltpu.VMEM((bm, d), _BF16)],
        compiler_params=_params(("arbitrary", "arbitrary")),
        name="inproj_even",
    )(x, g, w_t, w_t, w_gate, b_gate)


def _inproj_odd_kernel(x_ref, g_ref, wb_ref, wc_ref, wh_ref, bg_ref, z_ref, hn_ref):
    j = pl.program_id(1)

    def project(hn):
        bg_ref[...] = _dot(hn, wb_ref[...]).astype(_BF16)
        z_ref[...] = _dot(hn, wc_ref[...]) * _dot(hn, wh_ref[...])

    @pl.when(j == 0)
    def _():
        hn = _rmsnorm_rows(x_ref[...], g_ref[...]).astype(_BF16)
        hn_ref[...] = hn
        project(hn)

    @pl.when(j > 0)
    def _():
        project(hn_ref[...])


def _inproj_odd(x, g, g_row, w, layer):
    m, d = x.shape
    c = w.shape[2] // 3
    bm, tn = _ROW_BLOCK, _COL_TILE
    nt = c // tn
    return pl.pallas_call(
        _inproj_odd_kernel,
        grid=(m // bm, nt),
        in_specs=[
            pl.BlockSpec((bm, d), lambda i, j: (i, 0)),
            _layer_spec(g, g_row),
            pl.BlockSpec((None, d, tn), lambda i, j: (layer, 0, j)),
            pl.BlockSpec((None, d, tn), lambda i, j: (layer, 0, j + nt)),
            pl.BlockSpec((None, d, tn), lambda i, j: (layer, 0, j + 2 * nt)),
        ],
        out_specs=[
            pl.BlockSpec((bm, tn), lambda i, j: (i, j)),
            pl.BlockSpec((bm, tn), lambda i, j: (i, j)),
        ],
        out_shape=[
            jax.ShapeDtypeStruct((m, c), _BF16),
            jax.ShapeDtypeStruct((m, c), _F32),
        ],
        scratch_shapes=[pltpu.VMEM((bm, d), _BF16)],
        compiler_params=_params(("arbitrary", "arbitrary")),
        name="inproj_odd",
    )(x, g, w, w, w)


def _outproj_kernel(a_ref, w_ref, x_ref, o_ref, wb_ref):
    @pl.when(pl.program_id(0) == 0)
    def _():
        wb_ref[...] = w_ref[...].astype(_BF16)

    o_ref[...] = x_ref[...] + _dot(a_ref[...], wb_ref[...])


def _resident_weight_spec(w, layer):
    return pl.BlockSpec((None,) + w.shape[1:], lambda i: (layer, 0, 0), pipeline_mode=pl.Buffered(1))


def _outproj_residual(a, w, layer, x):
    m, k = a.shape
    n = w.shape[2]
    bm = _ROW_BLOCK // 2
    return pl.pallas_call(
        _outproj_kernel,
        grid=(m // bm,),
        in_specs=[
            pl.BlockSpec((bm, k), lambda i: (i, 0)),
            _resident_weight_spec(w, layer),
            pl.BlockSpec((bm, n), lambda i: (i, 0)),
        ],
        out_specs=pl.BlockSpec((bm, n), lambda i: (i, 0)),
        out_shape=jax.ShapeDtypeStruct((m, n), _F32),
        scratch_shapes=[pltpu.VMEM((k, n), _BF16)],
        compiler_params=_params(("arbitrary",)),
        name="outproj_residual",
    )(a, w, x)


def _ffn_kernel(x_ref, g_ref, wup_ref, wdn_ref, *rest, tail_start):
    final = len(rest) == 4
    if final:
        gfin_ref, o_ref, tail_ref, hn_ref = rest
    else:
        o_ref, hn_ref = rest
    i, j = pl.program_id(0), pl.program_id(1)

    def mlp_tile(hn):
        h = _dot(hn, wup_ref[...].astype(_BF16))
        h = jnp.square(jnp.maximum(h, 0.0)).astype(_BF16)
        return _dot(h, wdn_ref[...].astype(_BF16))

    @pl.when(j == 0)
    def _():
        x = x_ref[...]
        hn = _rmsnorm_rows(x, g_ref[...]).astype(_BF16)
        hn_ref[...] = hn
        o_ref[...] = x + mlp_tile(hn)

    @pl.when(j > 0)
    def _():
        o_ref[...] += mlp_tile(hn_ref[...])

    if final:
        @pl.when(j == pl.num_programs(1) - 1)
        def _():
            o_ref[...] = _rmsnorm_rows(o_ref[...], gfin_ref[...])

            @pl.when(i == pl.num_programs(0) - 1)
            def _():
                tail_ref[...] = o_ref[tail_start:, :]


def _ffn_residual(x, g, w_up, w_down, layer, final_gain=None, tail_rows=0):
    m, d = x.shape
    f = w_up.shape[2]
    bm, tf = _ROW_BLOCK, _COL_TILE
    in_specs = [
        pl.BlockSpec((bm, d), lambda i, j: (i, 0)),
        _layer_spec(g, layer),
        pl.BlockSpec((None, d, tf), lambda i, j: (layer, 0, j)),
        pl.BlockSpec((None, tf, d), lambda i, j: (layer, j, 0)),
    ]
    args = [x, g, w_up, w_down]
    row_block = pl.BlockSpec((bm, d), lambda i, j: (i, 0))
    if final_gain is None:
        out_specs = row_block
        out_shape = jax.ShapeDtypeStruct((m, d), _F32)
    else:
        assert tail_rows <= bm and (bm - tail_rows) % 8 == 0
        in_specs.append(_layer_spec(final_gain, 0))
        args.append(final_gain)
        out_specs = [row_block, pl.BlockSpec((tail_rows, d), lambda i, j: (0, 0))]
        out_shape = [jax.ShapeDtypeStruct((m - tail_rows, d), _F32),
                     jax.ShapeDtypeStruct((tail_rows, d), _F32)]
    return pl.pallas_call(
        functools.partial(_ffn_kernel, tail_start=bm - tail_rows),
        grid=(m // bm, f // tf),
        in_specs=in_specs,
        out_specs=out_specs,
        out_shape=out_shape,
        scratch_shapes=[pltpu.VMEM((bm, d), _BF16)],
        compiler_params=_params(("arbitrary", "arbitrary"), _VMEM_LIMIT_FFN),
        name="ffn_residual" if final_gain is None else "ffn_final",
    )(*args)


def _split3(x):
    hi = x.astype(_BF16)
    r1 = x - hi.astype(_F32)
    mid = r1.astype(_BF16)
    lo = (r1 - mid.astype(_F32)).astype(_BF16)
    return hi, mid, lo


def _mixer_even_kernel(*refs, rows, a_chunk, a_seq, seqs_per_group, carry_state):
    (u_ref, v_ref, q_ref, k_ref, vb_ref, r_ref, la_ref, wmix_ref, bcol_ref, gout_ref) = refs[:10]
    rest = refs[10:]
    if carry_state:
        mix_ref, sout_ref = rest[-2:]
        sin_ref = None
    else:
        sin_ref = rest[0]
        mix_ref, sout_ref = rest[-2:]

    tt = lax.broadcasted_iota(jnp.int32, (a_chunk, a_chunk), 0)
    ss = lax.broadcasted_iota(jnp.int32, (a_chunk, a_chunk), 1)
    a_mask = (ss <= tt) & ((tt // a_seq) == (ss // a_seq))
    for h in range(_HEADS):
        cols = slice(h * _A_HEAD, (h + 1) * _A_HEAD)
        wm = jnp.where(a_mask, wmix_ref[h], 0.0).astype(_BF16)
        bias = bcol_ref[h]
        for c in range(rows // a_chunk):
            rws = slice(c * a_chunk, (c + 1) * a_chunk)
            mixed = _dot(wm, v_ref[rws, cols]) + bias
            mix_ref[rws, cols] = (u_ref[rws, cols].astype(_F32) * mixed).astype(_BF16)

    gr = _GLA_ROWS
    seq_rows = gr // seqs_per_group
    ti = lax.broadcasted_iota(jnp.int32, (gr, gr), 0)
    si = lax.broadcasted_iota(jnp.int32, (gr, gr), 1)
    same_seq = (ti // seq_rows) == (si // seq_rows)
    causal = (si <= ti) & same_seq
    cum_and_total = jnp.concatenate(
        [jnp.where(causal, 1.0, 0.0), jnp.where(same_seq, 1.0, 0.0)], axis=0).astype(_BF16)
    kw = _HEADS * _DK
    row_k = lax.broadcasted_iota(jnp.int32, (gr, _DK), 0)
    row_v = lax.broadcasted_iota(jnp.int32, (gr, _DV), 0)
    pos_all = lax.broadcasted_iota(jnp.int32, (gr, kw), 0) % seq_rows
    ones_v = jnp.ones((gr, _DV), _BF16)
    scale = _DK ** -0.5
    a_width = _HEADS * _A_HEAD

    if carry_state:
        @pl.when(pl.program_id(1) == 0)
        def _():
            sout_ref[...] = jnp.zeros(sout_ref.shape, _F32)
        states = [sout_ref[h] for h in range(_HEADS)]

    for gi in range(rows // gr):
        rws = slice(gi * gr, (gi + 1) * gr)
        la = la_ref[rws, :]
        la_hi = la.astype(_BF16)
        la_lo = (la - la_hi.astype(_F32)).astype(_BF16)
        sums = _dot(cum_and_total, jnp.concatenate([la_hi, la_lo], axis=1))
        g_cum = sums[:gr, :kw] + sums[:gr, kw:]
        g_tot = sums[gr:, :kw] + sums[gr:, kw:]
        q = q_ref[rws, :].astype(_F32) * scale
        k = k_ref[rws, :].astype(_F32)
        q_in_all = (q * jnp.exp(g_cum)).astype(_BF16)
        k_in_all = (k * jnp.exp(-g_cum)).astype(_BF16)
        k_dec_all = (k * jnp.exp(g_tot - g_cum)).astype(_BF16)
        t_hi, t_mid, t_lo = _split3(g_tot)
        pieces_all = jnp.where(pos_all == 0, t_hi,
                               jnp.where(pos_all == 1, t_mid,
                                         jnp.where(pos_all == 2, t_lo, jnp.zeros_like(t_hi))))
        for h in range(_HEADS):
            kc = slice(h * _DK, (h + 1) * _DK)
            vc = slice(h * _DV, (h + 1) * _DV)
            q_in, k_in, k_dec, pieces = q_in_all[:, kc], k_in_all[:, kc], k_dec_all[:, kc], pieces_all[:, kc]
            v = vb_ref[rws, vc]
            scores = jnp.where(causal, _dot_nt(q_in, k_in), 0.0).astype(_BF16)
            if carry_state:
                o = _dot(jnp.concatenate([q_in, scores], axis=1),
                         jnp.concatenate([states[h].astype(_BF16), v], axis=0))
                decay = jnp.exp(_dot_tn(pieces, ones_v))
                states[h] = states[h] * decay + _dot_tn(k_dec, v)
            else:
                o = _dot(scores, v)
                zero = jnp.zeros_like(pieces)
                for s in range(seqs_per_group):
                    in_seq_k = (row_k // seq_rows) == s
                    in_seq_v = (row_v // seq_rows) == s
                    s0 = sin_ref[s, h]
                    o = o + jnp.where(in_seq_v, _dot(q_in, s0.astype(_BF16)), 0.0)
                    decay = jnp.exp(_dot_tn(jnp.where(in_seq_k, pieces, zero), ones_v))
                    kd = jnp.where(in_seq_k, k_dec, zero)
                    sout_ref[s, h] = s0 * decay + _dot_tn(kd, v)
            ms = jnp.mean(o * o, axis=-1, keepdims=True)
            on = (o * lax.rsqrt(ms + _EPS)) * gout_ref[:, vc]
            r = r_ref[rws, vc].astype(_F32)
            gated = on * (r * (1.0 / (1.0 + jnp.exp(-r))))
            mix_ref[rws, a_width + h * _DV:a_width + (h + 1) * _DV] = gated.astype(_BF16)

    if carry_state:
        for h in range(_HEADS):
            sout_ref[h] = states[h]


def _mixer_even(proj, loga, wmix, bcol, gout, *, layer_slot, n_slots, total_rows,
                row_start, n_seqs, seq_len, state_in=None, mix_prev=None, state_prev=None):
    carry = state_in is None
    a_width = _HEADS * _A_HEAD
    kw = _HEADS * _DK
    if carry:
        rows = 512
        blocks_per_seq = seq_len // rows
        grid = (n_seqs, blocks_per_seq)
        first = row_start // rows
        rmap = lambda col: (lambda n, c: (first + n * blocks_per_seq + c, col))
        a_chunk, a_seq, spg = _A_CHUNK, _A_CHUNK, 1
        state_shape = (n_slots, n_seqs, _HEADS, _DK, _DV)
        state_spec = pl.BlockSpec((None, None, _HEADS, _DK, _DV),
                                  lambda n, c: (layer_slot, n, 0, 0, 0))
        sem = ("arbitrary", "arbitrary")
    else:
        rows = _GLA_ROWS
        spg = rows // seq_len
        grid = (n_seqs // spg,)
        first = row_start // rows
        rmap = lambda col: (lambda g: (first + g, col))
        a_chunk, a_seq = rows, seq_len
        state_shape = (n_slots, n_seqs, _HEADS, _DK, _DV)
        state_spec = pl.BlockSpec((None, spg, _HEADS, _DK, _DV),
                                  lambda g: (layer_slot, g, 0, 0, 0))
        sem = ("arbitrary",)
    in_specs = [
        pl.BlockSpec((rows, a_width), rmap(0)),
        pl.BlockSpec((rows, a_width), rmap(1)),
        pl.BlockSpec((rows, kw), rmap(2 * a_width // kw)),
        pl.BlockSpec((rows, kw), rmap(2 * a_width // kw + 1)),
        pl.BlockSpec((rows, a_width), rmap(3)),
        pl.BlockSpec((rows, a_width), rmap(4)),
        pl.BlockSpec((rows, kw), rmap(0)),
        _layer_spec(wmix, layer_slot),
        _layer_spec(bcol, layer_slot),
        _layer_spec(gout, layer_slot),
    ]
    args = [proj, proj, proj, proj, proj, proj, loga, wmix, bcol, gout]
    if not carry:
        in_specs.append(state_spec)
        args.append(state_in)
    aliases = {}
    if mix_prev is not None:
        aliases[len(args)] = 0
        in_specs.append(pl.BlockSpec(memory_space=pl.ANY))
        args.append(mix_prev)
    if state_prev is not None:
        aliases[len(args)] = 1
        in_specs.append(pl.BlockSpec(memory_space=pl.ANY))
        args.append(state_prev)
    return pl.pallas_call(
        functools.partial(_mixer_even_kernel, rows=rows, a_chunk=a_chunk, a_seq=a_seq,
                          seqs_per_group=spg, carry_state=carry),
        grid=grid,
        in_specs=in_specs,
        out_specs=[pl.BlockSpec((rows, 2 * a_width), rmap(0)), state_spec],
        out_shape=[jax.ShapeDtypeStruct((total_rows, 2 * a_width), _BF16),
                   jax.ShapeDtypeStruct(state_shape, _F32)],
        input_output_aliases=aliases,
        compiler_params=_params(sem),
        name="mixer_even_long" if carry else "mixer_even_short",
    )(*args)


def _conv_outproj_kernel(z_ref, halo_ref, bg_ref, cw_ref, hist_ref, w_ref, x_ref, o_ref,
                         zs_ref, h1_ref, h2_ref, wb_ref,
                         *, rows, rows_long, seq_long, seq_short):
    i = pl.program_id(0)
    cols = z_ref.shape[1]
    keep = _CONV_TAPS - 1

    @pl.when(i == 0)
    def _():
        wb_ref[...] = w_ref[...].astype(_BF16)
        h1_ref[...] = jnp.zeros(h1_ref.shape, _F32)
        h2_ref[...] = jnp.zeros(h2_ref.shape, _F32)

    @pl.when((i + 1) * rows > rows_long)
    def _():
        nh = hist_ref.shape[0]
        r = i * rows + lax.broadcasted_iota(jnp.int32, (rows, nh), 0) - rows_long
        c = lax.broadcasted_iota(jnp.int32, (rows, nh), 1)
        rr = jnp.maximum(r, 0)
        n, t = rr // seq_short, rr % seq_short
        pieces = _split3(hist_ref[...])
        for back, h_ref in ((1, h1_ref), (2, h2_ref)):
            pick = (r >= 0) & (t < back) & (c == keep * n + keep - back + t)
            sel = jnp.where(pick, 1.0, 0.0).astype(_BF16)
            h_ref[...] = _dot(sel, pieces[0]) + _dot(sel, pieces[1]) + _dot(sel, pieces[2])

    zs_ref[8:8 + rows, :] = z_ref[...]
    zs_ref[0:8, :] = halo_ref[...]
    lanes = 128
    row = i * rows + lax.broadcasted_iota(jnp.int32, (rows, lanes), 0)
    t = jnp.where(row >= rows_long, (row - rows_long) % seq_short, row % seq_long)
    has_prev1, has_prev2 = t >= 1, t >= 2

    def gated(sl):
        prev1 = jnp.where(has_prev1, zs_ref[pl.ds(7, rows), sl], h1_ref[:, sl])
        prev2 = jnp.where(has_prev2, zs_ref[pl.ds(6, rows), sl], h2_ref[:, sl])
        conv = prev2 * cw_ref[0:1, sl] + prev1 * cw_ref[1:2, sl] + z_ref[:, sl] * cw_ref[2:3, sl]
        return (bg_ref[:, sl].astype(_F32) * conv).astype(_BF16)

    chunk = 2 * lanes
    acc = x_ref[...]
    for kb in range(cols // chunk):
        lo = kb * chunk
        g = jnp.concatenate([gated(slice(lo, lo + lanes)), gated(slice(lo + lanes, lo + chunk))], axis=1)
        acc = acc + _dot(g, wb_ref[lo:lo + chunk, :])
    o_ref[...] = acc


def _conv_outproj_residual(z, bg, cw, hist, w, layer, x, *, rows_long, seq_long, seq_short):
    m, c = z.shape
    n = w.shape[2]
    rows = _ROW_BLOCK // 4
    assert m % rows == 0 and rows % 8 == 0 and _CONV_TAPS == 3
    row_block = lambda width: pl.BlockSpec((rows, width), lambda i: (i, 0))
    return pl.pallas_call(
        functools.partial(_conv_outproj_kernel, rows=rows, rows_long=rows_long,
                          seq_long=seq_long, seq_short=seq_short),
        grid=(m // rows,),
        in_specs=[
            row_block(c),
            pl.BlockSpec((8, c), lambda i: (jnp.maximum(i * (rows // 8) - 1, 0), 0)),
            row_block(c),
            _layer_spec(cw, layer),
            _layer_spec(hist, layer),
            _resident_weight_spec(w, layer),
            row_block(n),
        ],
        out_specs=row_block(n),
        out_shape=jax.ShapeDtypeStruct((m, n), _F32),
        scratch_shapes=[pltpu.VMEM((rows + 8, c), _F32),
                        pltpu.VMEM((rows, c), _F32),
                        pltpu.VMEM((rows, c), _F32),
                        pltpu.VMEM((c, n), _BF16)],
        compiler_params=_params(("arbitrary",)),
        name="conv_outproj_residual",
    )(z, z, bg, cw, hist, w, x)


def kernel(x_prompt, x_sample, state_gla, state_conv, norm_mix, norm_ffn, norm_final, w_in_even, w_gate_up, b_gate, w_spatial, b_spatial, g_gla_out, w_out_even, w_in_odd, conv_w, w_out_odd, w_ffn_up, w_ffn_down):
    n_p, t_p, d = x_prompt.shape
    n_s, t_s, _ = x_sample.shape
    rows_p, rows_s = n_p * t_p, n_s * t_s
    total = rows_p + rows_s
    depth = norm_mix.shape[0]
    n_even = w_in_even.shape[0]
    a_width = _HEADS * _A_HEAD
    main_cols = w_in_even.shape[2] - _GATE_RANK

    x = jnp.concatenate([x_prompt.reshape(rows_p, d), x_sample.reshape(rows_s, d)], axis=0)

    reps = _GLA_ROWS // t_s
    wmix_s = jnp.tile(w_spatial[:, :, :t_s, :t_s], (1, 1, reps, reps))
    bcol_s = jnp.tile(b_spatial[:, :, :t_s], (1, 1, reps))[..., None]
    bcol_p = b_spatial[..., None]
    gain_mix, gain_ffn = norm_mix[:, None, :], norm_ffn[:, None, :]
    gain_final = norm_final[None, None, :]
    gain_gla = g_gla_out[:, None, :]
    bias_gate = b_gate[:, None, :]
    keep = _CONV_TAPS - 1
    hist = state_conv.astype(_F32).reshape(state_conv.shape[0], n_s * keep, d)

    w_in_even_t = jnp.swapaxes(w_in_even, 1, 2).astype(_BF16)
    w_in_odd_b = w_in_odd.astype(_BF16)
    w_gate = w_gate_up.astype(_BF16)

    v_rows, conv_p, conv_s = [], [], []
    gla_p = gla_s = None
    for l in range(depth):
        i = l // 2
        if l % 2 == 0:
            proj, v32, loga = _inproj_even(x, gain_mix, l, w_in_even_t, i, w_gate, bias_gate,
                                           tail_rows=rows_s)
            mix, gla_p = _mixer_even(
                proj, loga, w_spatial, bcol_p, gain_gla, layer_slot=i, n_slots=n_even,
                total_rows=total, row_start=0, n_seqs=n_p, seq_len=t_p, state_prev=gla_p)
            mix, gla_s = _mixer_even(
                proj, loga, wmix_s, bcol_s, gain_gla, layer_slot=i, n_slots=n_even,
                total_rows=total, row_start=rows_p, n_seqs=n_s, seq_len=t_s,
                state_in=state_gla, mix_prev=mix, state_prev=gla_s)
            v_rows.append(v32.reshape(n_s, t_s, a_width))
            x = _outproj_residual(mix, w_out_even, i, x)
        else:
            bg, z = _inproj_odd(x, gain_mix, l, w_in_odd_b, i)
            conv_p.append(jnp.stack([z[(n + 1) * t_p - keep:(n + 1) * t_p] for n in range(n_p)]))
            conv_s.append(z[rows_p:].reshape(n_s, t_s, d)[:, t_s - keep:])
            x = _conv_outproj_residual(z, bg, conv_w, hist, w_out_odd, i, x,
                                       rows_long=rows_p, seq_long=t_p, seq_short=t_s)
        if l < depth - 1:
            x = _ffn_residual(x, gain_ffn, w_ffn_up, w_ffn_down, l)
        else:
            y_p, y_s = _ffn_residual(x, gain_ffn, w_ffn_up, w_ffn_down, l,
                                     final_gain=gain_final, tail_rows=rows_s)

    return (y_p.reshape(n_p, t_p, d), y_s.reshape(n_s, t_s, d), gla_p, gla_s,
            jnp.stack(conv_p), jnp.stack(conv_s), jnp.stack(v_rows))
```

```python
import functools

import jax
import jax.numpy as jnp
from jax import lax
from jax.experimental import pallas as pl
from jax.experimental.pallas import tpu as pltpu

_F32 = jnp.float32
_BF16 = jnp.bfloat16

_EPS = 1e-6
_HEADS = 4
_A_HEAD = 256
_A_CHUNK = 128
_DK = 128
_DV = 256
_GATE_RANK = 16
_GATE_TAU = 16.0
_GLA_ROWS = 64
_CONV_TAPS = 3

_ROW_BLOCK = 1088
_COL_TILE = 512
_MIB = 1024 * 1024
_VMEM_LIMIT = 56 * _MIB
_VMEM_LIMIT_FFN = 62 * _MIB


def _params(semantics, vmem_limit=_VMEM_LIMIT):
    return pltpu.CompilerParams(dimension_semantics=semantics, vmem_limit_bytes=vmem_limit)


def _layer_spec(stacked, layer):
    tail = (0,) * (stacked.ndim - 1)
    return pl.BlockSpec((None,) + stacked.shape[1:], lambda *_: (layer,) + tail)


def _rmsnorm_rows(x, g):
    ms = jnp.mean(x * x, axis=-1, keepdims=True)
    return (x * lax.rsqrt(ms + _EPS)) * g


def _gelu_tanh(x):
    return x * (0.5 * (1.0 + jnp.tanh(0.7978845608028654 * (x + 0.044715 * (x * x * x)))))


def _log_sigmoid(x):
    return jnp.minimum(x, 0.0) - jnp.log(1.0 + jnp.exp(-jnp.abs(x)))


def _dot(a, b):
    return jnp.dot(a, b, preferred_element_type=_F32)


def _dot_nt(a, b):
    return lax.dot_general(a, b, (((1,), (1,)), ((), ())), preferred_element_type=_F32)


def _dot_tn(a, b):
    return lax.dot_general(a, b, (((0,), (0,)), ((), ())), preferred_element_type=_F32)


def _inproj_even_kernel(x_ref, g_ref, w_ref, wglr_ref, wgate_ref, bgate_ref,
                        proj_ref, v32_ref, loga_ref, hn_ref, *, gelu_tiles, v_tiles, tail_start):
    i, j = pl.program_id(0), pl.program_id(1)

    def project(hn, activate):
        acc = _dot_nt(hn, w_ref[...])
        proj_ref[...] = (_gelu_tanh(acc) if activate else acc).astype(_BF16)

    @pl.when(j == 0)
    def _():
        hn = _rmsnorm_rows(x_ref[...], g_ref[...]).astype(_BF16)
        hn_ref[...] = hn
        project(hn, True)

    @pl.when((j > 0) & (j < gelu_tiles))
    def _():
        project(hn_ref[...], True)

    @pl.when(j == gelu_tiles)
    def _():
        hn = hn_ref[...]
        glr = _dot_nt(hn, wglr_ref[...])
        gate = _dot(glr.astype(_BF16), wgate_ref[...]) + bgate_ref[...]
        loga_ref[...] = _log_sigmoid(gate) * (1.0 / _GATE_TAU)
        project(hn, False)

    @pl.when(j > gelu_tiles)
    def _():
        project(hn_ref[...], False)

    @pl.when((i == pl.num_programs(0) - 1) & (j >= v_tiles[0]) & (j < v_tiles[1]))
    def _():
        v32_ref[...] = _gelu_tanh(_dot_nt(hn_ref[tail_start:, :], w_ref[...]))


def _inproj_even(x, g, g_row, w_t, layer, w_gate, b_gate, tail_rows):
    m, d = x.shape
    n = w_t.shape[1] - _GATE_RANK
    assert n % _GATE_RANK == 0
    bm, tn = _ROW_BLOCK, 2 * _COL_TILE
    assert tail_rows <= bm and (bm - tail_rows) % 16 == 0
    a_width = _HEADS * _A_HEAD
    gelu_tiles = 2 * a_width // tn
    v_tiles = (a_width // tn, 2 * a_width // tn)
    nk = w_gate.shape[2]

    def v32_map(i, j):
        tile = jnp.clip(j - v_tiles[0], 0, v_tiles[1] - v_tiles[0] - 1)
        return (0, jnp.where(i == m // bm - 1, tile, 0))

    return pl.pallas_call(
        functools.partial(_inproj_even_kernel, gelu_tiles=gelu_tiles, v_tiles=v_tiles,
                          tail_start=bm - tail_rows),
        grid=(m // bm, n // tn),
        in_specs=[
            pl.BlockSpec((bm, d), lambda i, j: (i, 0)),
            _layer_spec(g, g_row),
            pl.BlockSpec((None, tn, d), lambda i, j: (layer, j, 0)),
            pl.BlockSpec((None, _GATE_RANK, d), lambda i, j: (layer, n // _GATE_RANK, 0)),
            _layer_spec(w_gate, layer),
            _layer_spec(b_gate, layer),
        ],
        out_specs=[
            pl.BlockSpec((bm, tn), lambda i, j: (i, j)),
            pl.BlockSpec((tail_rows, tn), v32_map),
            pl.BlockSpec((bm, nk), lambda i, j: (i, 0)),
        ],
        out_shape=[
            jax.ShapeDtypeStruct((m, n), _BF16),
            jax.ShapeDtypeStruct((tail_rows, a_width), _F32),
            jax.ShapeDtypeStruct((m, nk), _F32),
        ],
        scratch_shapes=[pltpu.VMEM((bm, d), _BF16)],
        compiler_params=_params(("arbitrary", "arbitrary")),
        name="inproj_even",
    )(x, g, w_t, w_t, w_gate, b_gate)


def _inproj_odd_kernel(x_ref, g_ref, wb_ref, wc_ref, wh_ref, bg_ref, z_ref, zb_ref, hn_ref):
    j = pl.program_id(1)

    def project(hn):
        bg_ref[...] = _dot(hn, wb_ref[...]).astype(_BF16)
        z = _dot(hn, wc_ref[...]) * _dot(hn, wh_ref[...])
        z_ref[...] = z
        zb_ref[...] = z.astype(_BF16)

    @pl.when(j == 0)
    def _():
        hn = _rmsnorm_rows(x_ref[...], g_ref[...]).astype(_BF16)
        hn_ref[...] = hn
        project(hn)

    @pl.when(j > 0)
    def _():
        project(hn_ref[...])


def _inproj_odd(x, g, g_row, w, layer):
    m, d = x.shape
    c = w.shape[2] // 3
    bm, tn = _ROW_BLOCK, _COL_TILE
    nt = c // tn
    return pl.pallas_call(
        _inproj_odd_kernel,
        grid=(m // bm, nt),
        in_specs=[
            pl.BlockSpec((bm, d), lambda i, j: (i, 0)),
            _layer_spec(g, g_row),
            pl.BlockSpec((None, d, tn), lambda i, j: (layer, 0, j)),
            pl.BlockSpec((None, d, tn), lambda i, j: (layer, 0, j + nt)),
            pl.BlockSpec((None, d, tn), lambda i, j: (layer, 0, j + 2 * nt)),
        ],
        out_specs=[pl.BlockSpec((bm, tn), lambda i, j: (i, j))] * 3,
        out_shape=[
            jax.ShapeDtypeStruct((m, c), _BF16),
            jax.ShapeDtypeStruct((m, c), _F32),
            jax.ShapeDtypeStruct((m, c), _BF16),
        ],
        scratch_shapes=[pltpu.VMEM((bm, d), _BF16)],
        compiler_params=_params(("arbitrary", "arbitrary")),
        name="inproj_odd",
    )(x, g, w, w, w)


def _outproj_kernel(a_ref, w_ref, x_ref, o_ref, wb_ref):
    @pl.when(pl.program_id(0) == 0)
    def _():
        wb_ref[...] = w_ref[...].astype(_BF16)

    o_ref[...] = x_ref[...] + _dot(a_ref[...], wb_ref[...])


def _resident_weight_spec(w, layer):
    return pl.BlockSpec((None,) + w.shape[1:], lambda i: (layer, 0, 0), pipeline_mode=pl.Buffered(1))


def _outproj_residual(a, w, layer, x):
    m, k = a.shape
    n = w.shape[2]
    bm = _ROW_BLOCK // 2
    return pl.pallas_call(
        _outproj_kernel,
        grid=(m // bm,),
        in_specs=[
            pl.BlockSpec((bm, k), lambda i: (i, 0)),
            _resident_weight_spec(w, layer),
            pl.BlockSpec((bm, n), lambda i: (i, 0)),
        ],
        out_specs=pl.BlockSpec((bm, n), lambda i: (i, 0)),
        out_shape=jax.ShapeDtypeStruct((m, n), _F32),
        scratch_shapes=[pltpu.VMEM((k, n), _BF16)],
        compiler_params=_params(("arbitrary",)),
        name="outproj_residual",
    )(a, w, x)


def _ffn_kernel(x_ref, g_ref, wup_ref, wdn_ref, *rest, tail_start):
    final = len(rest) == 4
    if final:
        gfin_ref, o_ref, tail_ref, hn_ref = rest
    else:
        o_ref, hn_ref = rest
    i, j = pl.program_id(0), pl.program_id(1)

    def mlp_tile(hn):
        h = _dot(hn, wup_ref[...].astype(_BF16))
        h = jnp.square(jnp.maximum(h, 0.0)).astype(_BF16)
        return _dot(h, wdn_ref[...].astype(_BF16))

    @pl.when(j == 0)
    def _():
        x = x_ref[...]
        hn = _rmsnorm_rows(x, g_ref[...]).astype(_BF16)
        hn_ref[...] = hn
        o_ref[...] = x + mlp_tile(hn)

    @pl.when(j > 0)
    def _():
        o_ref[...] += mlp_tile(hn_ref[...])

    if final:
        @pl.when(j == pl.num_programs(1) - 1)
        def _():
            o_ref[...] = _rmsnorm_rows(o_ref[...], gfin_ref[...])

            @pl.when(i == pl.num_programs(0) - 1)
            def _():
                tail_ref[...] = o_ref[tail_start:, :]


def _ffn_residual(x, g, w_up, w_down, layer, final_gain=None, tail_rows=0):
    m, d = x.shape
    f = w_up.shape[2]
    bm, tf = _ROW_BLOCK, _COL_TILE
    in_specs = [
        pl.BlockSpec((bm, d), lambda i, j: (i, 0)),
        _layer_spec(g, layer),
        pl.BlockSpec((None, d, tf), lambda i, j: (layer, 0, j)),
        pl.BlockSpec((None, tf, d), lambda i, j: (layer, j, 0)),
    ]
    args = [x, g, w_up, w_down]
    row_block = pl.BlockSpec((bm, d), lambda i, j: (i, 0))
    if final_gain is None:
        out_specs = row_block
        out_shape = jax.ShapeDtypeStruct((m, d), _F32)
    else:
        assert tail_rows <= bm and (bm - tail_rows) % 8 == 0
        in_specs.append(_layer_spec(final_gain, 0))
        args.append(final_gain)
        out_specs = [row_block, pl.BlockSpec((tail_rows, d), lambda i, j: (0, 0))]
        out_shape = [jax.ShapeDtypeStruct((m - tail_rows, d), _F32),
                     jax.ShapeDtypeStruct((tail_rows, d), _F32)]
    return pl.pallas_call(
        functools.partial(_ffn_kernel, tail_start=bm - tail_rows),
        grid=(m // bm, f // tf),
        in_specs=in_specs,
        out_specs=out_specs,
        out_shape=out_shape,
        scratch_shapes=[pltpu.VMEM((bm, d), _BF16)],
        compiler_params=_params(("arbitrary", "arbitrary"), _VMEM_LIMIT_FFN),
        name="ffn_residual" if final_gain is None else "ffn_final",
    )(*args)


def _split3(x):
    hi = x.astype(_BF16)
    r1 = x - hi.astype(_F32)
    mid = r1.astype(_BF16)
    lo = (r1 - mid.astype(_F32)).astype(_BF16)
    return hi, mid, lo


def _mixer_even_kernel(*refs, rows, a_chunk, a_seq, seqs_per_group, carry_state):
    (u_ref, v_ref, q_ref, k_ref, vb_ref, r_ref, la_ref, wmix_ref, bcol_ref, gout_ref) = refs[:10]
    rest = refs[10:]
    if carry_state:
        mix_ref, sout_ref = rest[-2:]
        sin_ref = None
    else:
        sin_ref = rest[0]
        mix_ref, sout_ref = rest[-2:]

    tt = lax.broadcasted_iota(jnp.int32, (a_chunk, a_chunk), 0)
    ss = lax.broadcasted_iota(jnp.int32, (a_chunk, a_chunk), 1)
    a_mask = (ss <= tt) & ((tt // a_seq) == (ss // a_seq))
    for h in range(_HEADS):
        cols = slice(h * _A_HEAD, (h + 1) * _A_HEAD)
        wm = jnp.where(a_mask, wmix_ref[h], 0.0).astype(_BF16)
        bias = bcol_ref[h]
        for c in range(rows // a_chunk):
            rws = slice(c * a_chunk, (c + 1) * a_chunk)
            mixed = _dot(wm, v_ref[rws, cols]) + bias
            mix_ref[rws, cols] = (u_ref[rws, cols].astype(_F32) * mixed).astype(_BF16)

    gr = _GLA_ROWS
    seq_rows = gr // seqs_per_group
    ti = lax.broadcasted_iota(jnp.int32, (gr, gr), 0)
    si = lax.broadcasted_iota(jnp.int32, (gr, gr), 1)
    same_seq = (ti // seq_rows) == (si // seq_rows)
    causal = (si <= ti) & same_seq
    cum_and_total = jnp.concatenate(
        [jnp.where(causal, 1.0, 0.0), jnp.where(same_seq, 1.0, 0.0)], axis=0).astype(_BF16)
    kw = _HEADS * _DK
    row_k = lax.broadcasted_iota(jnp.int32, (gr, _DK), 0)
    row_v = lax.broadcasted_iota(jnp.int32, (gr, _DV), 0)
    pos_all = lax.broadcasted_iota(jnp.int32, (gr, kw), 0) % seq_rows
    ones_v = jnp.ones((gr, _DV), _BF16)
    scale = _DK ** -0.5
    a_width = _HEADS * _A_HEAD

    if carry_state:
        @pl.when(pl.program_id(1) == 0)
        def _():
            sout_ref[...] = jnp.zeros(sout_ref.shape, _F32)
        states = [sout_ref[h] for h in range(_HEADS)]

    for gi in range(rows // gr):
        rws = slice(gi * gr, (gi + 1) * gr)
        la = la_ref[rws, :]
        la_hi = la.astype(_BF16)
        la_lo = (la - la_hi.astype(_F32)).astype(_BF16)
        sums = _dot(cum_and_total, jnp.concatenate([la_hi, la_lo], axis=1))
        g_cum = sums[:gr, :kw] + sums[:gr, kw:]
        g_tot = sums[gr:, :kw] + sums[gr:, kw:]
        q = q_ref[rws, :].astype(_F32) * scale
        k = k_ref[rws, :].astype(_F32)
        q_in_all = (q * jnp.exp(g_cum)).astype(_BF16)
        k_in_all = (k * jnp.exp(-g_cum)).astype(_BF16)
        k_dec_all = (k * jnp.exp(g_tot - g_cum)).astype(_BF16)
        t_hi, t_mid, t_lo = _split3(g_tot)
        pieces_all = jnp.where(pos_all == 0, t_hi,
                               jnp.where(pos_all == 1, t_mid,
                                         jnp.where(pos_all == 2, t_lo, jnp.zeros_like(t_hi))))
        for h in range(_HEADS):
            kc = slice(h * _DK, (h + 1) * _DK)
            vc = slice(h * _DV, (h + 1) * _DV)
            q_in, k_in, k_dec, pieces = q_in_all[:, kc], k_in_all[:, kc], k_dec_all[:, kc], pieces_all[:, kc]
            v = vb_ref[rws, vc]
            scores = jnp.where(causal, _dot_nt(q_in, k_in), 0.0).astype(_BF16)
            if carry_state:
                o = _dot(jnp.concatenate([q_in, scores], axis=1),
                         jnp.concatenate([states[h].astype(_BF16), v], axis=0))
                decay = jnp.exp(_dot_tn(pieces, ones_v))
                states[h] = states[h] * decay + _dot_tn(k_dec, v)
            else:
                o = _dot(scores, v)
                zero = jnp.zeros_like(pieces)
                for s in range(seqs_per_group):
                    in_seq_k = (row_k // seq_rows) == s
                    in_seq_v = (row_v // seq_rows) == s
                    s0 = sin_ref[s, h]
                    o = o + jnp.where(in_seq_v, _dot(q_in, s0.astype(_BF16)), 0.0)
                    decay = jnp.exp(_dot_tn(jnp.where(in_seq_k, pieces, zero), ones_v))
                    kd = jnp.where(in_seq_k, k_dec, zero)
                    sout_ref[s, h] = s0 * decay + _dot_tn(kd, v)
            ms = jnp.mean(o * o, axis=-1, keepdims=True)
            on = (o * lax.rsqrt(ms + _EPS)) * gout_ref[:, vc]
            r = r_ref[rws, vc].astype(_F32)
            gated = on * (r * (1.0 / (1.0 + jnp.exp(-r))))
            mix_ref[rws, a_width + h * _DV:a_width + (h + 1) * _DV] = gated.astype(_BF16)

    if carry_state:
        for h in range(_HEADS):
            sout_ref[h] = states[h]


def _mixer_even(proj, loga, wmix, bcol, gout, *, layer_slot, n_slots, total_rows,
                row_start, n_seqs, seq_len, state_in=None, mix_prev=None, state_prev=None):
    carry = state_in is None
    a_width = _HEADS * _A_HEAD
    kw = _HEADS * _DK
    if carry:
        rows = 512
        blocks_per_seq = seq_len // rows
        grid = (n_seqs, blocks_per_seq)
        first = row_start // rows
        rmap = lambda col: (lambda n, c: (first + n * blocks_per_seq + c, col))
        a_chunk, a_seq, spg = _A_CHUNK, _A_CHUNK, 1
        state_shape = (n_slots, n_seqs, _HEADS, _DK, _DV)
        state_spec = pl.BlockSpec((None, None, _HEADS, _DK, _DV),
                                  lambda n, c: (layer_slot, n, 0, 0, 0))
        sem = ("arbitrary", "arbitrary")
    else:
        rows = _GLA_ROWS
        spg = rows // seq_len
        grid = (n_seqs // spg,)
        first = row_start // rows
        rmap = lambda col: (lambda g: (first + g, col))
        a_chunk, a_seq = rows, seq_len
        state_shape = (n_slots, n_seqs, _HEADS, _DK, _DV)
        state_spec = pl.BlockSpec((None, spg, _HEADS, _DK, _DV),
                                  lambda g: (layer_slot, g, 0, 0, 0))
        sem = ("arbitrary",)
    in_specs = [
        pl.BlockSpec((rows, a_width), rmap(0)),
        pl.BlockSpec((rows, a_width), rmap(1)),
        pl.BlockSpec((rows, kw), rmap(2 * a_width // kw)),
        pl.BlockSpec((rows, kw), rmap(2 * a_width // kw + 1)),
        pl.BlockSpec((rows, a_width), rmap(3)),
        pl.BlockSpec((rows, a_width), rmap(4)),
        pl.BlockSpec((rows, kw), rmap(0)),
        _layer_spec(wmix, layer_slot),
        _layer_spec(bcol, layer_slot),
        _layer_spec(gout, layer_slot),
    ]
    args = [proj, proj, proj, proj, proj, proj, loga, wmix, bcol, gout]
    if not carry:
        in_specs.append(state_spec)
        args.append(state_in)
    aliases = {}
    if mix_prev is not None:
        aliases[len(args)] = 0
        in_specs.append(pl.BlockSpec(memory_space=pl.ANY))
        args.append(mix_prev)
    if state_prev is not None:
        aliases[len(args)] = 1
        in_specs.append(pl.BlockSpec(memory_space=pl.ANY))
        args.append(state_prev)
    return pl.pallas_call(
        functools.partial(_mixer_even_kernel, rows=rows, a_chunk=a_chunk, a_seq=a_seq,
                          seqs_per_group=spg, carry_state=carry),
        grid=grid,
        in_specs=in_specs,
        out_specs=[pl.BlockSpec((rows, 2 * a_width), rmap(0)), state_spec],
        out_shape=[jax.ShapeDtypeStruct((total_rows, 2 * a_width), _BF16),
                   jax.ShapeDtypeStruct(state_shape, _F32)],
        input_output_aliases=aliases,
        compiler_params=_params(sem),
        name="mixer_even_long" if carry else "mixer_even_short",
    )(*args)


def _conv_outproj_kernel(z_ref, halo_ref, bg_ref, cw_ref, hist_ref, w_ref, x_ref, o_ref,
                         zs_ref, h1_ref, h2_ref, wb_ref,
                         *, rows, rows_long, seq_long, seq_short):
    i = pl.program_id(0)
    cols = z_ref.shape[1]
    keep = _CONV_TAPS - 1

    @pl.when(i == 0)
    def _():
        wb_ref[...] = w_ref[...].astype(_BF16)
        h1_ref[...] = jnp.zeros(h1_ref.shape, _F32)
        h2_ref[...] = jnp.zeros(h2_ref.shape, _F32)

    @pl.when((i + 1) * rows > rows_long)
    def _():
        nh = hist_ref.shape[0]
        r = i * rows + lax.broadcasted_iota(jnp.int32, (rows, nh), 0) - rows_long
        c = lax.broadcasted_iota(jnp.int32, (rows, nh), 1)
        rr = jnp.maximum(r, 0)
        n, t = rr // seq_short, rr % seq_short
        pieces = _split3(hist_ref[...])
        for back, h_ref in ((1, h1_ref), (2, h2_ref)):
            pick = (r >= 0) & (t < back) & (c == keep * n + keep - back + t)
            sel = jnp.where(pick, 1.0, 0.0).astype(_BF16)
            h_ref[...] = _dot(sel, pieces[0]) + _dot(sel, pieces[1]) + _dot(sel, pieces[2])

    halo = halo_ref.shape[0]
    zs_ref[halo:halo + rows, :] = z_ref[...].astype(_F32)
    zs_ref[0:halo, :] = halo_ref[...].astype(_F32)
    lanes = 128
    row = i * rows + lax.broadcasted_iota(jnp.int32, (rows, lanes), 0)
    t = jnp.where(row >= rows_long, (row - rows_long) % seq_short, row % seq_long)
    has_prev1, has_prev2 = t >= 1, t >= 2

    def gated(sl):
        prev1 = jnp.where(has_prev1, zs_ref[pl.ds(halo - 1, rows), sl], h1_ref[:, sl])
        prev2 = jnp.where(has_prev2, zs_ref[pl.ds(halo - 2, rows), sl], h2_ref[:, sl])
        cur = zs_ref[pl.ds(halo, rows), sl]
        conv = prev2 * cw_ref[0:1, sl] + prev1 * cw_ref[1:2, sl] + cur * cw_ref[2:3, sl]
        return (bg_ref[:, sl].astype(_F32) * conv).astype(_BF16)

    chunk = 2 * lanes
    acc = x_ref[...]
    for kb in range(cols // chunk):
        lo = kb * chunk
        g = jnp.concatenate([gated(slice(lo, lo + lanes)), gated(slice(lo + lanes, lo + chunk))], axis=1)
        acc = acc + _dot(g, wb_ref[lo:lo + chunk, :])
    o_ref[...] = acc


def _conv_outproj_residual(z, bg, cw, hist, w, layer, x, *, rows_long, seq_long, seq_short):
    m, c = z.shape
    n = w.shape[2]
    rows = _ROW_BLOCK // 4
    halo = 16
    assert m % rows == 0 and rows % halo == 0 and _CONV_TAPS == 3
    row_block = lambda width: pl.BlockSpec((rows, width), lambda i: (i, 0))
    return pl.pallas_call(
        functools.partial(_conv_outproj_kernel, rows=rows, rows_long=rows_long,
                          seq_long=seq_long, seq_short=seq_short),
        grid=(m // rows,),
        in_specs=[
            row_block(c),
            pl.BlockSpec((halo, c), lambda i: (jnp.maximum(i * (rows // halo) - 1, 0), 0)),
            row_block(c),
            _layer_spec(cw, layer),
            _layer_spec(hist, layer),
            _resident_weight_spec(w, layer),
            row_block(n),
        ],
        out_specs=row_block(n),
        out_shape=jax.ShapeDtypeStruct((m, n), _F32),
        scratch_shapes=[pltpu.VMEM((rows + halo, c), _F32),
                        pltpu.VMEM((rows, c), _F32),
                        pltpu.VMEM((rows, c), _F32),
                        pltpu.VMEM((c, n), _BF16)],
        compiler_params=_params(("arbitrary",)),
        name="conv_outproj_residual",
    )(z, z, bg, cw, hist, w, x)


def kernel(x_prompt, x_sample, state_gla, state_conv, norm_mix, norm_ffn, norm_final, w_in_even, w_gate_up, b_gate, w_spatial, b_spatial, g_gla_out, w_out_even, w_in_odd, conv_w, w_out_odd, w_ffn_up, w_ffn_down):
    n_p, t_p, d = x_prompt.shape
    n_s, t_s, _ = x_sample.shape
    rows_p, rows_s = n_p * t_p, n_s * t_s
    total = rows_p + rows_s
    depth = norm_mix.shape[0]
    n_even = w_in_even.shape[0]
    a_width = _HEADS * _A_HEAD

    x = jnp.concatenate([x_prompt.reshape(rows_p, d), x_sample.reshape(rows_s, d)], axis=0)

    reps = _GLA_ROWS // t_s
    wmix_s = jnp.tile(w_spatial[:, :, :t_s, :t_s], (1, 1, reps, reps))
    bcol_s = jnp.tile(b_spatial[:, :, :t_s], (1, 1, reps))[..., None]
    bcol_p = b_spatial[..., None]
    gain_mix, gain_ffn = norm_mix[:, None, :], norm_ffn[:, None, :]
    gain_final = norm_final[None, None, :]
    gain_gla = g_gla_out[:, None, :]
    bias_gate = b_gate[:, None, :]
    keep = _CONV_TAPS - 1
    hist = state_conv.astype(_F32).reshape(state_conv.shape[0], n_s * keep, d)

    w_in_even_t = jnp.swapaxes(w_in_even, 1, 2).astype(_BF16)
    w_in_odd_b = w_in_odd.astype(_BF16)
    w_gate = w_gate_up.astype(_BF16)

    v_rows, conv_p, conv_s = [], [], []
    gla_p = gla_s = None
    for l in range(depth):
        i = l // 2
        if l % 2 == 0:
            proj, v32, loga = _inproj_even(x, gain_mix, l, w_in_even_t, i, w_gate, bias_gate,
                                           tail_rows=rows_s)
            mix, gla_p = _mixer_even(
                proj, loga, w_spatial, bcol_p, gain_gla, layer_slot=i, n_slots=n_even,
                total_rows=total, row_start=0, n_seqs=n_p, seq_len=t_p, state_prev=gla_p)
            mix, gla_s = _mixer_even(
                proj, loga, wmix_s, bcol_s, gain_gla, layer_slot=i, n_slots=n_even,
                total_rows=total, row_start=rows_p, n_seqs=n_s, seq_len=t_s,
                state_in=state_gla, mix_prev=mix, state_prev=gla_s)
            v_rows.append(v32.reshape(n_s, t_s, a_width))
            x = _outproj_residual(mix, w_out_even, i, x)
        else:
            bg, z, z_b = _inproj_odd(x, gain_mix, l, w_in_odd_b, i)
            conv_p.append(jnp.stack([z[(n + 1) * t_p - keep:(n + 1) * t_p] for n in range(n_p)]))
            conv_s.append(z[rows_p:].reshape(n_s, t_s, d)[:, t_s - keep:])
            x = _conv_outproj_residual(z_b, bg, conv_w, hist, w_out_odd, i, x,
                                       rows_long=rows_p, seq_long=t_p, seq_short=t_s)
        if l < depth - 1:
            x = _ffn_residual(x, gain_ffn, w_ffn_up, w_ffn_down, l)
        else:
            y_p, y_s = _ffn_residual(x, gain_ffn, w_ffn_up, w_ffn_down, l,
                                     final_gain=gain_final, tail_rows=rows_s)

    return (y_p.reshape(n_p, t_p, d), y_s.reshape(n_s, t_s, d), gla_p, gla_s,
            jnp.stack(conv_p), jnp.stack(conv_s), jnp.stack(v_rows))
```

```python
import functools

import jax
import jax.numpy as jnp
from jax import lax
from jax.experimental import pallas as pl
from jax.experimental.pallas import tpu as pltpu

_F32 = jnp.float32
_BF16 = jnp.bfloat16

_EPS = 1e-6
_HEADS = 4
_A_HEAD = 256
_A_CHUNK = 128
_DK = 128
_DV = 256
_GATE_RANK = 16
_GATE_TAU = 16.0
_GLA_ROWS = 64
_CONV_TAPS = 3

_ROW_BLOCK = 1088
_COL_TILE = 512
_MIB = 1024 * 1024
_VMEM_LIMIT = 56 * _MIB
_VMEM_LIMIT_FFN = 62 * _MIB


def _params(semantics, vmem_limit=_VMEM_LIMIT):
    return pltpu.CompilerParams(dimension_semantics=semantics, vmem_limit_bytes=vmem_limit)


def _layer_spec(stacked, layer):
    tail = (0,) * (stacked.ndim - 1)
    return pl.BlockSpec((None,) + stacked.shape[1:], lambda *_: (layer,) + tail)


def _rmsnorm_rows(x, g):
    ms = jnp.mean(x * x, axis=-1, keepdims=True)
    return (x * lax.rsqrt(ms + _EPS)) * g


def _gelu_tanh(x):
    return x * (0.5 * (1.0 + jnp.tanh(0.7978845608028654 * (x + 0.044715 * (x * x * x)))))


def _log_sigmoid(x):
    return jnp.minimum(x, 0.0) - jnp.log(1.0 + jnp.exp(-jnp.abs(x)))


def _dot(a, b):
    return jnp.dot(a, b, preferred_element_type=_F32)


def _dot_nt(a, b):
    return lax.dot_general(a, b, (((1,), (1,)), ((), ())), preferred_element_type=_F32)


def _dot_tn(a, b):
    return lax.dot_general(a, b, (((0,), (0,)), ((), ())), preferred_element_type=_F32)


def _inproj_even_kernel(*refs, gelu_tiles, v_tiles, tail_start, split_rows):
    refs = list(refs)
    x_ref = refs.pop(0)
    xt_ref = refs.pop(0) if split_rows else None
    g_ref, w_ref, wglr_ref, wgate_ref, bgate_ref, proj_ref, v32_ref, loga_ref, hn_ref = refs
    i, j = pl.program_id(0), pl.program_id(1)
    last = pl.num_programs(0) - 1

    def project(hn, activate):
        acc = _dot_nt(hn, w_ref[...])
        proj_ref[...] = (_gelu_tanh(acc) if activate else acc).astype(_BF16)

    if split_rows:
        @pl.when((j == 0) & (i < last))
        def _():
            hn_ref[...] = _rmsnorm_rows(x_ref[...], g_ref[...]).astype(_BF16)

        @pl.when((j == 0) & (i == last))
        def _():
            hn_ref[:tail_start, :] = _rmsnorm_rows(x_ref[:tail_start, :], g_ref[...]).astype(_BF16)
            hn_ref[tail_start:, :] = _rmsnorm_rows(xt_ref[...], g_ref[...]).astype(_BF16)

        @pl.when(j == 0)
        def _():
            project(hn_ref[...], True)
    else:
        @pl.when(j == 0)
        def _():
            hn = _rmsnorm_rows(x_ref[...], g_ref[...]).astype(_BF16)
            hn_ref[...] = hn
            project(hn, True)

    @pl.when((j > 0) & (j < gelu_tiles))
    def _():
        project(hn_ref[...], True)

    @pl.when(j == gelu_tiles)
    def _():
        hn = hn_ref[...]
        glr = _dot_nt(hn, wglr_ref[...])
        gate = _dot(glr.astype(_BF16), wgate_ref[...]) + bgate_ref[...]
        loga_ref[...] = _log_sigmoid(gate) * (1.0 / _GATE_TAU)
        project(hn, False)

    @pl.when(j > gelu_tiles)
    def _():
        project(hn_ref[...], False)

    @pl.when((i == last) & (j >= v_tiles[0]) & (j < v_tiles[1]))
    def _():
        v32_ref[...] = _gelu_tanh(_dot_nt(hn_ref[tail_start:, :], w_ref[...]))


def _inproj_even(x, g, g_row, w_t, layer, w_gate, b_gate, tail_rows, x_tail=None):
    d = x.shape[1]
    m = x.shape[0] + (0 if x_tail is None else x_tail.shape[0])
    n = w_t.shape[1] - _GATE_RANK
    assert n % _GATE_RANK == 0
    bm, tn = _ROW_BLOCK, 2 * _COL_TILE
    assert tail_rows <= bm and (bm - tail_rows) % 16 == 0
    a_width = _HEADS * _A_HEAD
    gelu_tiles = 2 * a_width // tn
    v_tiles = (a_width // tn, 2 * a_width // tn)
    nk = w_gate.shape[2]

    def v32_map(i, j):
        tile = jnp.clip(j - v_tiles[0], 0, v_tiles[1] - v_tiles[0] - 1)
        return (0, jnp.where(i == m // bm - 1, tile, 0))

    x_specs, x_args = [pl.BlockSpec((bm, d), lambda i, j: (i, 0))], [x]
    if x_tail is not None:
        assert x_tail.shape[0] == tail_rows
        x_specs.append(pl.BlockSpec((tail_rows, d), lambda i, j: (0, 0), pipeline_mode=pl.Buffered(1)))
        x_args.append(x_tail)
    return pl.pallas_call(
        functools.partial(_inproj_even_kernel, gelu_tiles=gelu_tiles, v_tiles=v_tiles,
                          tail_start=bm - tail_rows, split_rows=x_tail is not None),
        grid=(m // bm, n // tn),
        in_specs=x_specs + [
            _layer_spec(g, g_row),
            pl.BlockSpec((None, tn, d), lambda i, j: (layer, j, 0)),
            pl.BlockSpec((None, _GATE_RANK, d), lambda i, j: (layer, n // _GATE_RANK, 0)),
            _layer_spec(w_gate, layer),
            _layer_spec(b_gate, layer),
        ],
        out_specs=[
            pl.BlockSpec((bm, tn), lambda i, j: (i, j)),
            pl.BlockSpec((tail_rows, tn), v32_map),
            pl.BlockSpec((bm, nk), lambda i, j: (i, 0)),
        ],
        out_shape=[
            jax.ShapeDtypeStruct((m, n), _BF16),
            jax.ShapeDtypeStruct((tail_rows, a_width), _F32),
            jax.ShapeDtypeStruct((m, nk), _F32),
        ],
        scratch_shapes=[pltpu.VMEM((bm, d), _BF16)],
        compiler_params=_params(("arbitrary", "arbitrary")),
        name="inproj_even",
    )(*x_args, g, w_t, w_t, w_gate, b_gate)


def _inproj_odd_kernel(x_ref, g_ref, wb_ref, wc_ref, wh_ref, bg_ref, z_ref, zb_ref, hn_ref):
    j = pl.program_id(1)

    def project(hn):
        bg_ref[...] = _dot(hn, wb_ref[...]).astype(_BF16)
        z = _dot(hn, wc_ref[...]) * _dot(hn, wh_ref[...])
        z_ref[...] = z
        zb_ref[...] = z.astype(_BF16)

    @pl.when(j == 0)
    def _():
        hn = _rmsnorm_rows(x_ref[...], g_ref[...]).astype(_BF16)
        hn_ref[...] = hn
        project(hn)

    @pl.when(j > 0)
    def _():
        project(hn_ref[...])


def _inproj_odd(x, g, g_row, w, layer):
    m, d = x.shape
    c = w.shape[2] // 3
    bm, tn = _ROW_BLOCK, _COL_TILE
    nt = c // tn
    return pl.pallas_call(
        _inproj_odd_kernel,
        grid=(m // bm, nt),
        in_specs=[
            pl.BlockSpec((bm, d), lambda i, j: (i, 0)),
            _layer_spec(g, g_row),
            pl.BlockSpec((None, d, tn), lambda i, j: (layer, 0, j)),
            pl.BlockSpec((None, d, tn), lambda i, j: (layer, 0, j + nt)),
            pl.BlockSpec((None, d, tn), lambda i, j: (layer, 0, j + 2 * nt)),
        ],
        out_specs=[pl.BlockSpec((bm, tn), lambda i, j: (i, j))] * 3,
        out_shape=[
            jax.ShapeDtypeStruct((m, c), _BF16),
            jax.ShapeDtypeStruct((m, c), _F32),
            jax.ShapeDtypeStruct((m, c), _BF16),
        ],
        scratch_shapes=[pltpu.VMEM((bm, d), _BF16)],
        compiler_params=_params(("arbitrary", "arbitrary")),
        name="inproj_odd",
    )(x, g, w, w, w)


def _outproj_kernel(*refs, head_rows):
    if head_rows is None:
        a_ref, w_ref, x_ref, o_ref, wb_ref = refs
    else:
        a_ref, w_ref, x_ref, xt_ref, o_ref, wb_ref = refs
    i = pl.program_id(0)

    @pl.when(i == 0)
    def _():
        wb_ref[...] = w_ref[...].astype(_BF16)

    if head_rows is None:
        o_ref[...] = x_ref[...] + _dot(a_ref[...], wb_ref[...])
    else:
        last = pl.num_programs(0) - 1

        @pl.when(i < last)
        def _():
            o_ref[...] = x_ref[...] + _dot(a_ref[...], wb_ref[...])

        @pl.when(i == last)
        def _():
            acc = _dot(a_ref[...], wb_ref[...])
            o_ref[:head_rows, :] = x_ref[:head_rows, :] + acc[:head_rows]
            o_ref[head_rows:, :] = xt_ref[...] + acc[head_rows:]


def _resident_weight_spec(w, layer):
    return pl.BlockSpec((None,) + w.shape[1:], lambda i: (layer, 0, 0), pipeline_mode=pl.Buffered(1))


def _outproj_residual(a, w, layer, x, x_tail=None):
    m, k = a.shape
    n = w.shape[2]
    bm = _ROW_BLOCK // 2
    in_specs = [
        pl.BlockSpec((bm, k), lambda i: (i, 0)),
        _resident_weight_spec(w, layer),
        pl.BlockSpec((bm, n), lambda i: (i, 0)),
    ]
    args = [a, w, x]
    head_rows = None
    if x_tail is not None:
        head_rows = x.shape[0] - (m // bm - 1) * bm
        assert x.shape[0] + x_tail.shape[0] == m and head_rows + x_tail.shape[0] == bm
        assert head_rows % 8 == 0 and head_rows > 0
        in_specs.append(pl.BlockSpec(x_tail.shape, lambda i: (0, 0), pipeline_mode=pl.Buffered(1)))
        args.append(x_tail)
    return pl.pallas_call(
        functools.partial(_outproj_kernel, head_rows=head_rows),
        grid=(m // bm,),
        in_specs=in_specs,
        out_specs=pl.BlockSpec((bm, n), lambda i: (i, 0)),
        out_shape=jax.ShapeDtypeStruct((m, n), _F32),
        scratch_shapes=[pltpu.VMEM((k, n), _BF16)],
        compiler_params=_params(("arbitrary",)),
        name="outproj_residual",
    )(*args)


def _ffn_kernel(x_ref, g_ref, wup_ref, wdn_ref, *rest, tail_start):
    final = len(rest) == 4
    if final:
        gfin_ref, o_ref, tail_ref, hn_ref = rest
    else:
        o_ref, hn_ref = rest
    i, j = pl.program_id(0), pl.program_id(1)

    def mlp_tile(hn):
        h = _dot(hn, wup_ref[...].astype(_BF16))
        h = jnp.square(jnp.maximum(h, 0.0)).astype(_BF16)
        return _dot(h, wdn_ref[...].astype(_BF16))

    @pl.when(j == 0)
    def _():
        x = x_ref[...]
        hn = _rmsnorm_rows(x, g_ref[...]).astype(_BF16)
        hn_ref[...] = hn
        o_ref[...] = x + mlp_tile(hn)

    @pl.when(j > 0)
    def _():
        o_ref[...] += mlp_tile(hn_ref[...])

    if final:
        @pl.when(j == pl.num_programs(1) - 1)
        def _():
            o_ref[...] = _rmsnorm_rows(o_ref[...], gfin_ref[...])

            @pl.when(i == pl.num_programs(0) - 1)
            def _():
                tail_ref[...] = o_ref[tail_start:, :]


def _ffn_residual(x, g, w_up, w_down, layer, final_gain=None, tail_rows=0):
    m, d = x.shape
    f = w_up.shape[2]
    bm, tf = _ROW_BLOCK, _COL_TILE
    in_specs = [
        pl.BlockSpec((bm, d), lambda i, j: (i, 0)),
        _layer_spec(g, layer),
        pl.BlockSpec((None, d, tf), lambda i, j: (layer, 0, j)),
        pl.BlockSpec((None, tf, d), lambda i, j: (layer, j, 0)),
    ]
    args = [x, g, w_up, w_down]
    row_block = pl.BlockSpec((bm, d), lambda i, j: (i, 0))
    if final_gain is None:
        out_specs = row_block
        out_shape = jax.ShapeDtypeStruct((m, d), _F32)
    else:
        assert tail_rows <= bm and (bm - tail_rows) % 8 == 0
        in_specs.append(_layer_spec(final_gain, 0))
        args.append(final_gain)
        out_specs = [row_block, pl.BlockSpec((tail_rows, d), lambda i, j: (0, 0))]
        out_shape = [jax.ShapeDtypeStruct((m - tail_rows, d), _F32),
                     jax.ShapeDtypeStruct((tail_rows, d), _F32)]
    return pl.pallas_call(
        functools.partial(_ffn_kernel, tail_start=bm - tail_rows),
        grid=(m // bm, f // tf),
        in_specs=in_specs,
        out_specs=out_specs,
        out_shape=out_shape,
        scratch_shapes=[pltpu.VMEM((bm, d), _BF16)],
        compiler_params=_params(("arbitrary", "arbitrary"), _VMEM_LIMIT_FFN),
        name="ffn_residual" if final_gain is None else "ffn_final",
    )(*args)


def _split3(x):
    hi = x.astype(_BF16)
    r1 = x - hi.astype(_F32)
    mid = r1.astype(_BF16)
    lo = (r1 - mid.astype(_F32)).astype(_BF16)
    return hi, mid, lo


def _mixer_even_kernel(*refs, rows, a_chunk, a_seq, seqs_per_group, carry_state):
    (u_ref, v_ref, q_ref, k_ref, vb_ref, r_ref, la_ref, wmix_ref, bcol_ref, gout_ref) = refs[:10]
    rest = refs[10:]
    if carry_state:
        mix_ref, sout_ref = rest[-2:]
        sin_ref = None
    else:
        sin_ref = rest[0]
        mix_ref, sout_ref = rest[-2:]

    tt = lax.broadcasted_iota(jnp.int32, (a_chunk, a_chunk), 0)
    ss = lax.broadcasted_iota(jnp.int32, (a_chunk, a_chunk), 1)
    a_mask = (ss <= tt) & ((tt // a_seq) == (ss // a_seq))
    for h in range(_HEADS):
        cols = slice(h * _A_HEAD, (h + 1) * _A_HEAD)
        wm = jnp.where(a_mask, wmix_ref[h], 0.0).astype(_BF16)
        bias = bcol_ref[h]
        for c in range(rows // a_chunk):
            rws = slice(c * a_chunk, (c + 1) * a_chunk)
            mixed = _dot(wm, v_ref[rws, cols]) + bias
            mix_ref[rws, cols] = (u_ref[rws, cols].astype(_F32) * mixed).astype(_BF16)

    gr = _GLA_ROWS
    seq_rows = gr // seqs_per_group
    ti = lax.broadcasted_iota(jnp.int32, (gr, gr), 0)
    si = lax.broadcasted_iota(jnp.int32, (gr, gr), 1)
    same_seq = (ti // seq_rows) == (si // seq_rows)
    causal = (si <= ti) & same_seq
    cum_and_total = jnp.concatenate(
        [jnp.where(causal, 1.0, 0.0), jnp.where(same_seq, 1.0, 0.0)], axis=0).astype(_BF16)
    kw = _HEADS * _DK
    row_k = lax.broadcasted_iota(jnp.int32, (gr, _DK), 0)
    row_v = lax.broadcasted_iota(jnp.int32, (gr, _DV), 0)
    pos_all = lax.broadcasted_iota(jnp.int32, (gr, kw), 0) % seq_rows
    ones_v = jnp.ones((gr, _DV), _BF16)
    scale = _DK ** -0.5
    a_width = _HEADS * _A_HEAD

    if carry_state:
        @pl.when(pl.program_id(1) == 0)
        def _():
            sout_ref[...] = jnp.zeros(sout_ref.shape, _F32)
        states = [sout_ref[h] for h in range(_HEADS)]

    for gi in range(rows // gr):
        rws = slice(gi * gr, (gi + 1) * gr)
        la = la_ref[rws, :]
        la_hi = la.astype(_BF16)
        la_lo = (la - la_hi.astype(_F32)).astype(_BF16)
        sums = _dot(cum_and_total, jnp.concatenate([la_hi, la_lo], axis=1))
        g_cum = sums[:gr, :kw] + sums[:gr, kw:]
        g_tot = sums[gr:, :kw] + sums[gr:, kw:]
        q = q_ref[rws, :].astype(_F32) * scale
        k = k_ref[rws, :].astype(_F32)
        q_in_all = (q * jnp.exp(g_cum)).astype(_BF16)
        k_in_all = (k * jnp.exp(-g_cum)).astype(_BF16)
        k_dec_all = (k * jnp.exp(g_tot - g_cum)).astype(_BF16)
        t_hi, t_mid, t_lo = _split3(g_tot)
        pieces_all = jnp.where(pos_all == 0, t_hi,
                               jnp.where(pos_all == 1, t_mid,
                                         jnp.where(pos_all == 2, t_lo, jnp.zeros_like(t_hi))))
        for h in range(_HEADS):
            kc = slice(h * _DK, (h + 1) * _DK)
            vc = slice(h * _DV, (h + 1) * _DV)
            q_in, k_in, k_dec, pieces = q_in_all[:, kc], k_in_all[:, kc], k_dec_all[:, kc], pieces_all[:, kc]
            v = vb_ref[rws, vc]
            scores = jnp.where(causal, _dot_nt(q_in, k_in), 0.0).astype(_BF16)
            if carry_state:
                o = _dot(jnp.concatenate([q_in, scores], axis=1),
                         jnp.concatenate([states[h].astype(_BF16), v], axis=0))
                decay = jnp.exp(_dot_tn(pieces, ones_v))
                states[h] = states[h] * decay + _dot_tn(k_dec, v)
            else:
                o = _dot(scores, v)
                zero = jnp.zeros_like(pieces)
                for s in range(seqs_per_group):
                    in_seq_k = (row_k // seq_rows) == s
                    in_seq_v = (row_v // seq_rows) == s
                    s0 = sin_ref[s, h]
                    o = o + jnp.where(in_seq_v, _dot(q_in, s0.astype(_BF16)), 0.0)
                    decay = jnp.exp(_dot_tn(jnp.where(in_seq_k, pieces, zero), ones_v))
                    kd = jnp.where(in_seq_k, k_dec, zero)
                    sout_ref[s, h] = s0 * decay + _dot_tn(kd, v)
            ms = jnp.mean(o * o, axis=-1, keepdims=True)
            on = (o * lax.rsqrt(ms + _EPS)) * gout_ref[:, vc]
            r = r_ref[rws, vc].astype(_F32)
            gated = on * (r * (1.0 / (1.0 + jnp.exp(-r))))
            mix_ref[rws, a_width + h * _DV:a_width + (h + 1) * _DV] = gated.astype(_BF16)

    if carry_state:
        for h in range(_HEADS):
            sout_ref[h] = states[h]


def _mixer_even(proj, loga, wmix, bcol, gout, *, layer_slot, n_slots, total_rows,
                row_start, n_seqs, seq_len, state_in=None, mix_prev=None, state_prev=None):
    carry = state_in is None
    a_width = _HEADS * _A_HEAD
    kw = _HEADS * _DK
    if carry:
        rows = 512
        blocks_per_seq = seq_len // rows
        grid = (n_seqs, blocks_per_seq)
        first = row_start // rows
        rmap = lambda col: (lambda n, c: (first + n * blocks_per_seq + c, col))
        a_chunk, a_seq, spg = _A_CHUNK, _A_CHUNK, 1
        state_shape = (n_slots, n_seqs, _HEADS, _DK, _DV)
        state_spec = pl.BlockSpec((None, None, _HEADS, _DK, _DV),
                                  lambda n, c: (layer_slot, n, 0, 0, 0))
        sem = ("arbitrary", "arbitrary")
    else:
        rows = _GLA_ROWS
        spg = rows // seq_len
        grid = (n_seqs // spg,)
        first = row_start // rows
        rmap = lambda col: (lambda g: (first + g, col))
        a_chunk, a_seq = rows, seq_len
        state_shape = (n_slots, n_seqs, _HEADS, _DK, _DV)
        state_spec = pl.BlockSpec((None, spg, _HEADS, _DK, _DV),
                                  lambda g: (layer_slot, g, 0, 0, 0))
        sem = ("arbitrary",)
    in_specs = [
        pl.BlockSpec((rows, a_width), rmap(0)),
        pl.BlockSpec((rows, a_width), rmap(1)),
        pl.BlockSpec((rows, kw), rmap(2 * a_width // kw)),
        pl.BlockSpec((rows, kw), rmap(2 * a_width // kw + 1)),
        pl.BlockSpec((rows, a_width), rmap(3)),
        pl.BlockSpec((rows, a_width), rmap(4)),
        pl.BlockSpec((rows, kw), rmap(0)),
        _layer_spec(wmix, layer_slot),
        _layer_spec(bcol, layer_slot),
        _layer_spec(gout, layer_slot),
    ]
    args = [proj, proj, proj, proj, proj, proj, loga, wmix, bcol, gout]
    if not carry:
        in_specs.append(state_spec)
        args.append(state_in)
    aliases = {}
    if mix_prev is not None:
        aliases[len(args)] = 0
        in_specs.append(pl.BlockSpec(memory_space=pl.ANY))
        args.append(mix_prev)
    if state_prev is not None:
        aliases[len(args)] = 1
        in_specs.append(pl.BlockSpec(memory_space=pl.ANY))
        args.append(state_prev)
    return pl.pallas_call(
        functools.partial(_mixer_even_kernel, rows=rows, a_chunk=a_chunk, a_seq=a_seq,
                          seqs_per_group=spg, carry_state=carry),
        grid=grid,
        in_specs=in_specs,
        out_specs=[pl.BlockSpec((rows, 2 * a_width), rmap(0)), state_spec],
        out_shape=[jax.ShapeDtypeStruct((total_rows, 2 * a_width), _BF16),
                   jax.ShapeDtypeStruct(state_shape, _F32)],
        input_output_aliases=aliases,
        compiler_params=_params(sem),
        name="mixer_even_long" if carry else "mixer_even_short",
    )(*args)


def _conv_outproj_kernel(z_ref, halo_ref, bg_ref, cw_ref, hist_ref, w_ref, x_ref, o_ref,
                         zs_ref, h1_ref, h2_ref, wb_ref,
                         *, rows, rows_long, seq_long, seq_short):
    i = pl.program_id(0)
    cols = z_ref.shape[1]
    keep = _CONV_TAPS - 1

    @pl.when(i == 0)
    def _():
        wb_ref[...] = w_ref[...].astype(_BF16)
        h1_ref[...] = jnp.zeros(h1_ref.shape, _F32)
        h2_ref[...] = jnp.zeros(h2_ref.shape, _F32)

    @pl.when((i + 1) * rows > rows_long)
    def _():
        nh = hist_ref.shape[0]
        r = i * rows + lax.broadcasted_iota(jnp.int32, (rows, nh), 0) - rows_long
        c = lax.broadcasted_iota(jnp.int32, (rows, nh), 1)
        rr = jnp.maximum(r, 0)
        n, t = rr // seq_short, rr % seq_short
        pieces = _split3(hist_ref[...])
        for back, h_ref in ((1, h1_ref), (2, h2_ref)):
            pick = (r >= 0) & (t < back) & (c == keep * n + keep - back + t)
            sel = jnp.where(pick, 1.0, 0.0).astype(_BF16)
            h_ref[...] = _dot(sel, pieces[0]) + _dot(sel, pieces[1]) + _dot(sel, pieces[2])

    halo = halo_ref.shape[0]
    zs_ref[halo:halo + rows, :] = z_ref[...].astype(_F32)
    zs_ref[0:halo, :] = halo_ref[...].astype(_F32)
    lanes = 128
    row = i * rows + lax.broadcasted_iota(jnp.int32, (rows, lanes), 0)
    t = jnp.where(row >= rows_long, (row - rows_long) % seq_short, row % seq_long)
    has_prev1, has_prev2 = t >= 1, t >= 2

    def gated(sl):
        prev1 = jnp.where(has_prev1, zs_ref[pl.ds(halo - 1, rows), sl], h1_ref[:, sl])
        prev2 = jnp.where(has_prev2, zs_ref[pl.ds(halo - 2, rows), sl], h2_ref[:, sl])
        cur = zs_ref[pl.ds(halo, rows), sl]
        conv = prev2 * cw_ref[0:1, sl] + prev1 * cw_ref[1:2, sl] + cur * cw_ref[2:3, sl]
        return (bg_ref[:, sl].astype(_F32) * conv).astype(_BF16)

    chunk = 2 * lanes
    acc = x_ref[...]
    for kb in range(cols // chunk):
        lo = kb * chunk
        g = jnp.concatenate([gated(slice(lo, lo + lanes)), gated(slice(lo + lanes, lo + chunk))], axis=1)
        acc = acc + _dot(g, wb_ref[lo:lo + chunk, :])
    o_ref[...] = acc


def _conv_outproj_residual(z, bg, cw, hist, w, layer, x, *, rows_long, seq_long, seq_short):
    m, c = z.shape
    n = w.shape[2]
    rows = _ROW_BLOCK // 4
    halo = 16
    assert m % rows == 0 and rows % halo == 0 and _CONV_TAPS == 3
    row_block = lambda width: pl.BlockSpec((rows, width), lambda i: (i, 0))
    return pl.pallas_call(
        functools.partial(_conv_outproj_kernel, rows=rows, rows_long=rows_long,
                          seq_long=seq_long, seq_short=seq_short),
        grid=(m // rows,),
        in_specs=[
            row_block(c),
            pl.BlockSpec((halo, c), lambda i: (jnp.maximum(i * (rows // halo) - 1, 0), 0)),
            row_block(c),
            _layer_spec(cw, layer),
            _layer_spec(hist, layer),
            _resident_weight_spec(w, layer),
            row_block(n),
        ],
        out_specs=row_block(n),
        out_shape=jax.ShapeDtypeStruct((m, n), _F32),
        scratch_shapes=[pltpu.VMEM((rows + halo, c), _F32),
                        pltpu.VMEM((rows, c), _F32),
                        pltpu.VMEM((rows, c), _F32),
                        pltpu.VMEM((c, n), _BF16)],
        compiler_params=_params(("arbitrary",)),
        name="conv_outproj_residual",
    )(z, z, bg, cw, hist, w, x)


def kernel(x_prompt, x_sample, state_gla, state_conv, norm_mix, norm_ffn, norm_final, w_in_even, w_gate_up, b_gate, w_spatial, b_spatial, g_gla_out, w_out_even, w_in_odd, conv_w, w_out_odd, w_ffn_up, w_ffn_down):
    n_p, t_p, d = x_prompt.shape
    n_s, t_s, _ = x_sample.shape
    rows_p, rows_s = n_p * t_p, n_s * t_s
    total = rows_p + rows_s
    depth = norm_mix.shape[0]
    n_even = w_in_even.shape[0]
    a_width = _HEADS * _A_HEAD

    x, x_tail = x_prompt.reshape(rows_p, d), x_sample.reshape(rows_s, d)

    reps = _GLA_ROWS // t_s
    wmix_s = jnp.tile(w_spatial[:, :, :t_s, :t_s], (1, 1, reps, reps))
    bcol_s = jnp.tile(b_spatial[:, :, :t_s], (1, 1, reps))[..., None]
    bcol_p = b_spatial[..., None]
    gain_mix, gain_ffn = norm_mix[:, None, :], norm_ffn[:, None, :]
    gain_final = norm_final[None, None, :]
    gain_gla = g_gla_out[:, None, :]
    bias_gate = b_gate[:, None, :]
    keep = _CONV_TAPS - 1
    hist = state_conv.astype(_F32).reshape(state_conv.shape[0], n_s * keep, d)

    w_in_even_t = jnp.swapaxes(w_in_even, 1, 2).astype(_BF16)
    w_in_odd_b = w_in_odd.astype(_BF16)
    w_gate = w_gate_up.astype(_BF16)

    v_rows, conv_p, conv_s = [], [], []
    gla_p = gla_s = None
    for l in range(depth):
        i = l // 2
        if l % 2 == 0:
            proj, v32, loga = _inproj_even(x, gain_mix, l, w_in_even_t, i, w_gate, bias_gate,
                                           tail_rows=rows_s, x_tail=x_tail)
            mix, gla_p = _mixer_even(
                proj, loga, w_spatial, bcol_p, gain_gla, layer_slot=i, n_slots=n_even,
                total_rows=total, row_start=0, n_seqs=n_p, seq_len=t_p, state_prev=gla_p)
            mix, gla_s = _mixer_even(
                proj, loga, wmix_s, bcol_s, gain_gla, layer_slot=i, n_slots=n_even,
                total_rows=total, row_start=rows_p, n_seqs=n_s, seq_len=t_s,
                state_in=state_gla, mix_prev=mix, state_prev=gla_s)
            v_rows.append(v32.reshape(n_s, t_s, a_width))
            x, x_tail = _outproj_residual(mix, w_out_even, i, x, x_tail), None
        else:
            bg, z, z_b = _inproj_odd(x, gain_mix, l, w_in_odd_b, i)
            conv_p.append(jnp.stack([z[(n + 1) * t_p - keep:(n + 1) * t_p] for n in range(n_p)]))
            conv_s.append(z[rows_p:].reshape(n_s, t_s, d)[:, t_s - keep:])
            x = _conv_outproj_residual(z_b, bg, conv_w, hist, w_out_odd, i, x,
                                       rows_long=rows_p, seq_long=t_p, seq_short=t_s)
        if l < depth - 1:
            x = _ffn_residual(x, gain_ffn, w_ffn_up, w_ffn_down, l)
        else:
            y_p, y_s = _ffn_residual(x, gain_ffn, w_ffn_up, w_ffn_down, l,
                                     final_gain=gain_final, tail_rows=rows_s)

    return (y_p.reshape(n_p, t_p, d), y_s.reshape(n_s, t_s, d), gla_p, gla_s,
            jnp.stack(conv_p), jnp.stack(conv_s), jnp.stack(v_rows))
```

```python
import functools

import jax
import jax.numpy as jnp
from jax import lax
from jax.experimental import pallas as pl
from jax.experimental.pallas import tpu as pltpu

_F32 = jnp.float32
_BF16 = jnp.bfloat16

_EPS = 1e-6
_HEADS = 4
_A_HEAD = 256
_A_CHUNK = 128
_DK = 128
_DV = 256
_GATE_RANK = 16
_GATE_TAU = 16.0
_GLA_ROWS = 64
_CONV_TAPS = 3

_ROW_BLOCK = 1088
_COL_TILE = 512
_MIB = 1024 * 1024
_VMEM_LIMIT = 56 * _MIB
_VMEM_LIMIT_FFN = 62 * _MIB


def _params(semantics, vmem_limit=_VMEM_LIMIT):
    return pltpu.CompilerParams(dimension_semantics=semantics, vmem_limit_bytes=vmem_limit)


def _layer_spec(stacked, layer):
    tail = (0,) * (stacked.ndim - 1)
    return pl.BlockSpec((None,) + stacked.shape[1:], lambda *_: (layer,) + tail)


def _rmsnorm_rows(x, g):
    ms = jnp.mean(x * x, axis=-1, keepdims=True)
    return (x * lax.rsqrt(ms + _EPS)) * g


def _gelu_tanh(x):
    return x * (0.5 * (1.0 + jnp.tanh(0.7978845608028654 * (x + 0.044715 * (x * x * x)))))


def _log_sigmoid(x):
    return jnp.minimum(x, 0.0) - jnp.log(1.0 + jnp.exp(-jnp.abs(x)))


def _dot(a, b):
    return jnp.dot(a, b, preferred_element_type=_F32)


def _dot_nt(a, b):
    return lax.dot_general(a, b, (((1,), (1,)), ((), ())), preferred_element_type=_F32)


def _dot_tn(a, b):
    return lax.dot_general(a, b, (((0,), (0,)), ((), ())), preferred_element_type=_F32)


def _inproj_even_kernel(*refs, gelu_tiles, v_tiles, tail_start, split_rows):
    refs = list(refs)
    x_ref = refs.pop(0)
    xt_ref = refs.pop(0) if split_rows else None
    g_ref, w_ref, wglr_ref, wgate_ref, bgate_ref, proj_ref, v32_ref, loga_ref, hn_ref = refs
    i, j = pl.program_id(0), pl.program_id(1)
    last = pl.num_programs(0) - 1

    def project(hn, activate):
        acc = _dot_nt(hn, w_ref[...])
        proj_ref[...] = (_gelu_tanh(acc) if activate else acc).astype(_BF16)

    if split_rows:
        @pl.when((j == 0) & (i < last))
        def _():
            hn_ref[...] = _rmsnorm_rows(x_ref[...], g_ref[...]).astype(_BF16)

        @pl.when((j == 0) & (i == last))
        def _():
            hn_ref[:tail_start, :] = _rmsnorm_rows(x_ref[:tail_start, :], g_ref[...]).astype(_BF16)
            hn_ref[tail_start:, :] = _rmsnorm_rows(xt_ref[...], g_ref[...]).astype(_BF16)

        @pl.when(j == 0)
        def _():
            project(hn_ref[...], True)
    else:
        @pl.when(j == 0)
        def _():
            hn = _rmsnorm_rows(x_ref[...], g_ref[...]).astype(_BF16)
            hn_ref[...] = hn
            project(hn, True)

    @pl.when((j > 0) & (j < gelu_tiles))
    def _():
        project(hn_ref[...], True)

    @pl.when(j == gelu_tiles)
    def _():
        hn = hn_ref[...]
        glr = _dot_nt(hn, wglr_ref[...])
        gate = _dot(glr.astype(_BF16), wgate_ref[...]) + bgate_ref[...]
        loga_ref[...] = _log_sigmoid(gate) * (1.0 / _GATE_TAU)
        project(hn, False)

    @pl.when(j > gelu_tiles)
    def _():
        project(hn_ref[...], False)

    @pl.when((i == last) & (j >= v_tiles[0]) & (j < v_tiles[1]))
    def _():
        v32_ref[...] = _gelu_tanh(_dot_nt(hn_ref[tail_start:, :], w_ref[...]))


def _inproj_even(x, g, g_row, w_t, layer, w_gate, b_gate, tail_rows, x_tail=None):
    d = x.shape[1]
    m = x.shape[0] + (0 if x_tail is None else x_tail.shape[0])
    n = w_t.shape[1] - _GATE_RANK
    assert n % _GATE_RANK == 0
    bm, tn = _ROW_BLOCK, 2 * _COL_TILE
    assert tail_rows <= bm and (bm - tail_rows) % 16 == 0
    a_width = _HEADS * _A_HEAD
    gelu_tiles = 2 * a_width // tn
    v_tiles = (a_width // tn, 2 * a_width // tn)
    nk = w_gate.shape[2]

    def v32_map(i, j):
        tile = jnp.clip(j - v_tiles[0], 0, v_tiles[1] - v_tiles[0] - 1)
        return (0, jnp.where(i == m // bm - 1, tile, 0))

    x_specs, x_args = [pl.BlockSpec((bm, d), lambda i, j: (i, 0))], [x]
    if x_tail is not None:
        assert x_tail.shape[0] == tail_rows
        x_specs.append(pl.BlockSpec((tail_rows, d), lambda i, j: (0, 0), pipeline_mode=pl.Buffered(1)))
        x_args.append(x_tail)
    return pl.pallas_call(
        functools.partial(_inproj_even_kernel, gelu_tiles=gelu_tiles, v_tiles=v_tiles,
                          tail_start=bm - tail_rows, split_rows=x_tail is not None),
        grid=(m // bm, n // tn),
        in_specs=x_specs + [
            _layer_spec(g, g_row),
            pl.BlockSpec((None, tn, d), lambda i, j: (layer, j, 0)),
            pl.BlockSpec((None, _GATE_RANK, d), lambda i, j: (layer, n // _GATE_RANK, 0)),
            _layer_spec(w_gate, layer),
            _layer_spec(b_gate, layer),
        ],
        out_specs=[
            pl.BlockSpec((bm, tn), lambda i, j: (i, j)),
            pl.BlockSpec((tail_rows, tn), v32_map),
            pl.BlockSpec((bm, nk), lambda i, j: (i, 0)),
        ],
        out_shape=[
            jax.ShapeDtypeStruct((m, n), _BF16),
            jax.ShapeDtypeStruct((tail_rows, a_width), _F32),
            jax.ShapeDtypeStruct((m, nk), _F32),
        ],
        scratch_shapes=[pltpu.VMEM((bm, d), _BF16)],
        compiler_params=_params(("arbitrary", "arbitrary")),
        name="inproj_even",
    )(*x_args, g, w_t, w_t, w_gate, b_gate)


def _inproj_odd_kernel(x_ref, g_ref, wb_ref, wc_ref, wh_ref, bg_ref, z_ref, zb_ref, hn_ref):
    j = pl.program_id(1)

    def project(hn):
        bg_ref[...] = _dot(hn, wb_ref[...]).astype(_BF16)
        z = _dot(hn, wc_ref[...]) * _dot(hn, wh_ref[...])
        z_ref[...] = z
        zb_ref[...] = z.astype(_BF16)

    @pl.when(j == 0)
    def _():
        hn = _rmsnorm_rows(x_ref[...], g_ref[...]).astype(_BF16)
        hn_ref[...] = hn
        project(hn)

    @pl.when(j > 0)
    def _():
        project(hn_ref[...])


def _inproj_odd(x, g, g_row, w, layer):
    m, d = x.shape
    c = w.shape[2] // 3
    bm, tn = _ROW_BLOCK, _COL_TILE
    nt = c // tn
    return pl.pallas_call(
        _inproj_odd_kernel,
        grid=(m // bm, nt),
        in_specs=[
            pl.BlockSpec((bm, d), lambda i, j: (i, 0)),
            _layer_spec(g, g_row),
            pl.BlockSpec((None, d, tn), lambda i, j: (layer, 0, j)),
            pl.BlockSpec((None, d, tn), lambda i, j: (layer, 0, j + nt)),
            pl.BlockSpec((None, d, tn), lambda i, j: (layer, 0, j + 2 * nt)),
        ],
        out_specs=[pl.BlockSpec((bm, tn), lambda i, j: (i, j))] * 3,
        out_shape=[
            jax.ShapeDtypeStruct((m, c), _BF16),
            jax.ShapeDtypeStruct((m, c), _F32),
            jax.ShapeDtypeStruct((m, c), _BF16),
        ],
        scratch_shapes=[pltpu.VMEM((bm, d), _BF16)],
        compiler_params=_params(("arbitrary", "arbitrary")),
        name="inproj_odd",
    )(x, g, w, w, w)


def _outproj_kernel(*refs, head_rows):
    if head_rows is None:
        a_ref, w_ref, x_ref, o_ref, wb_ref = refs
    else:
        a_ref, w_ref, x_ref, xt_ref, o_ref, wb_ref = refs
    i = pl.program_id(0)

    @pl.when(i == 0)
    def _():
        wb_ref[...] = w_ref[...].astype(_BF16)

    if head_rows is None:
        o_ref[...] = x_ref[...] + _dot(a_ref[...], wb_ref[...])
    else:
        last = pl.num_programs(0) - 1

        @pl.when(i < last)
        def _():
            o_ref[...] = x_ref[...] + _dot(a_ref[...], wb_ref[...])

        @pl.when(i == last)
        def _():
            acc = _dot(a_ref[...], wb_ref[...])
            o_ref[:head_rows, :] = x_ref[:head_rows, :] + acc[:head_rows]
            o_ref[head_rows:, :] = xt_ref[...] + acc[head_rows:]


def _resident_weight_spec(w, layer):
    return pl.BlockSpec((None,) + w.shape[1:], lambda i: (layer, 0, 0), pipeline_mode=pl.Buffered(1))


def _outproj_residual(a, w, layer, x, x_tail=None):
    m, k = a.shape
    n = w.shape[2]
    bm = _ROW_BLOCK // 2
    in_specs = [
        pl.BlockSpec((bm, k), lambda i: (i, 0)),
        _resident_weight_spec(w, layer),
        pl.BlockSpec((bm, n), lambda i: (i, 0)),
    ]
    args = [a, w, x]
    head_rows = None
    if x_tail is not None:
        head_rows = x.shape[0] - (m // bm - 1) * bm
        assert x.shape[0] + x_tail.shape[0] == m and head_rows + x_tail.shape[0] == bm
        assert head_rows % 8 == 0 and head_rows > 0
        in_specs.append(pl.BlockSpec(x_tail.shape, lambda i: (0, 0), pipeline_mode=pl.Buffered(1)))
        args.append(x_tail)
    return pl.pallas_call(
        functools.partial(_outproj_kernel, head_rows=head_rows),
        grid=(m // bm,),
        in_specs=in_specs,
        out_specs=pl.BlockSpec((bm, n), lambda i: (i, 0)),
        out_shape=jax.ShapeDtypeStruct((m, n), _F32),
        scratch_shapes=[pltpu.VMEM((k, n), _BF16)],
        compiler_params=_params(("arbitrary",)),
        name="outproj_residual",
    )(*args)


def _ffn_kernel(x_ref, g_ref, wup_ref, wdn_ref, *rest, tail_start):
    final = len(rest) == 4
    if final:
        gfin_ref, o_ref, tail_ref, hn_ref = rest
    else:
        o_ref, hn_ref = rest
    i, j = pl.program_id(0), pl.program_id(1)

    def mlp_tile(hn):
        h = _dot(hn, wup_ref[...].astype(_BF16))
        h = jnp.square(jnp.maximum(h, 0.0)).astype(_BF16)
        return _dot(h, wdn_ref[...].astype(_BF16))

    @pl.when(j == 0)
    def _():
        x = x_ref[...]
        hn = _rmsnorm_rows(x, g_ref[...]).astype(_BF16)
        hn_ref[...] = hn
        o_ref[...] = x + mlp_tile(hn)

    @pl.when(j > 0)
    def _():
        o_ref[...] += mlp_tile(hn_ref[...])

    if final:
        @pl.when(j == pl.num_programs(1) - 1)
        def _():
            o_ref[...] = _rmsnorm_rows(o_ref[...], gfin_ref[...])

            @pl.when(i == pl.num_programs(0) - 1)
            def _():
                tail_ref[...] = o_ref[tail_start:, :]


def _ffn_residual(x, g, w_up, w_down, layer, final_gain=None, tail_rows=0):
    m, d = x.shape
    f = w_up.shape[2]
    bm, tf = _ROW_BLOCK, _COL_TILE
    in_specs = [
        pl.BlockSpec((bm, d), lambda i, j: (i, 0)),
        _layer_spec(g, layer),
        pl.BlockSpec((None, d, tf), lambda i, j: (layer, 0, j)),
        pl.BlockSpec((None, tf, d), lambda i, j: (layer, j, 0)),
    ]
    args = [x, g, w_up, w_down]
    row_block = pl.BlockSpec((bm, d), lambda i, j: (i, 0))
    if final_gain is None:
        out_specs = row_block
        out_shape = jax.ShapeDtypeStruct((m, d), _F32)
    else:
        assert tail_rows <= bm and (bm - tail_rows) % 8 == 0
        in_specs.append(_layer_spec(final_gain, 0))
        args.append(final_gain)
        out_specs = [row_block, pl.BlockSpec((tail_rows, d), lambda i, j: (0, 0))]
        out_shape = [jax.ShapeDtypeStruct((m - tail_rows, d), _F32),
                     jax.ShapeDtypeStruct((tail_rows, d), _F32)]
    return pl.pallas_call(
        functools.partial(_ffn_kernel, tail_start=bm - tail_rows),
        grid=(m // bm, f // tf),
        in_specs=in_specs,
        out_specs=out_specs,
        out_shape=out_shape,
        scratch_shapes=[pltpu.VMEM((bm, d), _BF16)],
        compiler_params=_params(("arbitrary", "arbitrary"), _VMEM_LIMIT_FFN),
        name="ffn_residual" if final_gain is None else "ffn_final",
    )(*args)


def _split3(x):
    hi = x.astype(_BF16)
    r1 = x - hi.astype(_F32)
    mid = r1.astype(_BF16)
    lo = (r1 - mid.astype(_F32)).astype(_BF16)
    return hi, mid, lo


def _mixer_even_kernel(*refs, rows, a_chunk, a_seq, seqs_per_group, carry_state):
    (u_ref, v_ref, q_ref, k_ref, vb_ref, r_ref, la_ref, wmix_ref, bcol_ref, gout_ref) = refs[:10]
    rest = refs[10:]
    if carry_state:
        mix_ref, sout_ref = rest[-2:]
        sin_ref = None
    else:
        sin_ref = rest[0]
        mix_ref, sout_ref = rest[-2:]

    tt = lax.broadcasted_iota(jnp.int32, (a_chunk, a_chunk), 0)
    ss = lax.broadcasted_iota(jnp.int32, (a_chunk, a_chunk), 1)
    a_mask = (ss <= tt) & ((tt // a_seq) == (ss // a_seq))
    for h in range(_HEADS):
        cols = slice(h * _A_HEAD, (h + 1) * _A_HEAD)
        wm = jnp.where(a_mask, wmix_ref[h], 0.0).astype(_BF16)
        bias = bcol_ref[h]
        for c in range(rows // a_chunk):
            rws = slice(c * a_chunk, (c + 1) * a_chunk)
            mixed = _dot(wm, v_ref[rws, cols]) + bias
            mix_ref[rws, cols] = (u_ref[rws, cols].astype(_F32) * mixed).astype(_BF16)

    gr = _GLA_ROWS
    seq_rows = gr // seqs_per_group
    ti = lax.broadcasted_iota(jnp.int32, (gr, gr), 0)
    si = lax.broadcasted_iota(jnp.int32, (gr, gr), 1)
    same_seq = (ti // seq_rows) == (si // seq_rows)
    causal = (si <= ti) & same_seq
    cum_and_total = jnp.concatenate(
        [jnp.where(causal, 1.0, 0.0), jnp.where(same_seq, 1.0, 0.0)], axis=0).astype(_BF16)
    kw = _HEADS * _DK
    row_k = lax.broadcasted_iota(jnp.int32, (gr, _DK), 0)
    row_v = lax.broadcasted_iota(jnp.int32, (gr, _DV), 0)
    pos_all = lax.broadcasted_iota(jnp.int32, (gr, kw), 0) % seq_rows
    ones_v = jnp.ones((gr, _DV), _BF16)
    scale = _DK ** -0.5
    a_width = _HEADS * _A_HEAD

    if carry_state:
        @pl.when(pl.program_id(1) == 0)
        def _():
            sout_ref[...] = jnp.zeros(sout_ref.shape, _F32)
        states = [sout_ref[h] for h in range(_HEADS)]

    for gi in range(rows // gr):
        rws = slice(gi * gr, (gi + 1) * gr)
        la = la_ref[rws, :]
        la_hi = la.astype(_BF16)
        la_lo = (la - la_hi.astype(_F32)).astype(_BF16)
        sums = _dot(cum_and_total, jnp.concatenate([la_hi, la_lo], axis=1))
        g_cum = sums[:gr, :kw] + sums[:gr, kw:]
        g_tot = sums[gr:, :kw] + sums[gr:, kw:]
        q = q_ref[rws, :].astype(_F32) * scale
        k = k_ref[rws, :].astype(_F32)
        q_in_all = (q * jnp.exp(g_cum)).astype(_BF16)
        k_in_all = (k * jnp.exp(-g_cum)).astype(_BF16)
        k_dec_all = (k * jnp.exp(g_tot - g_cum)).astype(_BF16)
        t_hi, t_mid, t_lo = _split3(g_tot)
        pieces_all = jnp.where(pos_all == 0, t_hi,
                               jnp.where(pos_all == 1, t_mid,
                                         jnp.where(pos_all == 2, t_lo, jnp.zeros_like(t_hi))))
        for h in range(_HEADS):
            kc = slice(h * _DK, (h + 1) * _DK)
            vc = slice(h * _DV, (h + 1) * _DV)
            q_in, k_in, k_dec, pieces = q_in_all[:, kc], k_in_all[:, kc], k_dec_all[:, kc], pieces_all[:, kc]
            v = vb_ref[rws, vc]
            scores = jnp.where(causal, _dot_nt(q_in, k_in), 0.0).astype(_BF16)
            if carry_state:
                o = _dot(jnp.concatenate([q_in, scores], axis=1),
                         jnp.concatenate([states[h].astype(_BF16), v], axis=0))
                decay = jnp.exp(_dot_tn(pieces, ones_v))
                states[h] = states[h] * decay + _dot_tn(k_dec, v)
            else:
                o = _dot(scores, v)
                zero = jnp.zeros_like(pieces)
                for s in range(seqs_per_group):
                    in_seq_k = (row_k // seq_rows) == s
                    in_seq_v = (row_v // seq_rows) == s
                    s0 = sin_ref[s, h]
                    o = o + jnp.where(in_seq_v, _dot(q_in, s0.astype(_BF16)), 0.0)
                    decay = jnp.exp(_dot_tn(jnp.where(in_seq_k, pieces, zero), ones_v))
                    kd = jnp.where(in_seq_k, k_dec, zero)
                    sout_ref[s, h] = s0 * decay + _dot_tn(kd, v)
            ms = jnp.mean(o * o, axis=-1, keepdims=True)
            on = (o * lax.rsqrt(ms + _EPS)) * gout_ref[:, vc]
            r = r_ref[rws, vc].astype(_F32)
            gated = on * (r * (1.0 / (1.0 + jnp.exp(-r))))
            mix_ref[rws, a_width + h * _DV:a_width + (h + 1) * _DV] = gated.astype(_BF16)

    if carry_state:
        for h in range(_HEADS):
            sout_ref[h] = states[h]


def _mixer_even(proj, loga, wmix, bcol, gout, *, layer_slot, n_slots, total_rows,
                row_start, n_seqs, seq_len, state_in=None, mix_prev=None, state_prev=None):
    carry = state_in is None
    a_width = _HEADS * _A_HEAD
    kw = _HEADS * _DK
    if carry:
        rows = 512
        blocks_per_seq = seq_len // rows
        grid = (n_seqs, blocks_per_seq)
        first = row_start // rows
        rmap = lambda col: (lambda n, c: (first + n * blocks_per_seq + c, col))
        a_chunk, a_seq, spg = _A_CHUNK, _A_CHUNK, 1
        state_shape = (n_slots, n_seqs, _HEADS, _DK, _DV)
        state_spec = pl.BlockSpec((None, None, _HEADS, _DK, _DV),
                                  lambda n, c: (layer_slot, n, 0, 0, 0))
        sem = ("arbitrary", "arbitrary")
    else:
        rows = _GLA_ROWS
        spg = rows // seq_len
        grid = (n_seqs // spg,)
        first = row_start // rows
        rmap = lambda col: (lambda g: (first + g, col))
        a_chunk, a_seq = rows, seq_len
        state_shape = (n_slots, n_seqs, _HEADS, _DK, _DV)
        state_spec = pl.BlockSpec((None, spg, _HEADS, _DK, _DV),
                                  lambda g: (layer_slot, g, 0, 0, 0))
        sem = ("arbitrary",)
    in_specs = [
        pl.BlockSpec((rows, a_width), rmap(0)),
        pl.BlockSpec((rows, a_width), rmap(1)),
        pl.BlockSpec((rows, kw), rmap(2 * a_width // kw)),
        pl.BlockSpec((rows, kw), rmap(2 * a_width // kw + 1)),
        pl.BlockSpec((rows, a_width), rmap(3)),
        pl.BlockSpec((rows, a_width), rmap(4)),
        pl.BlockSpec((rows, kw), rmap(0)),
        _layer_spec(wmix, layer_slot),
        _layer_spec(bcol, layer_slot),
        _layer_spec(gout, layer_slot),
    ]
    args = [proj, proj, proj, proj, proj, proj, loga, wmix, bcol, gout]
    if not carry:
        in_specs.append(state_spec)
        args.append(state_in)
    aliases = {}
    if mix_prev is not None:
        aliases[len(args)] = 0
        in_specs.append(pl.BlockSpec(memory_space=pl.ANY))
        args.append(mix_prev)
    if state_prev is not None:
        aliases[len(args)] = 1
        in_specs.append(pl.BlockSpec(memory_space=pl.ANY))
        args.append(state_prev)
    return pl.pallas_call(
        functools.partial(_mixer_even_kernel, rows=rows, a_chunk=a_chunk, a_seq=a_seq,
                          seqs_per_group=spg, carry_state=carry),
        grid=grid,
        in_specs=in_specs,
        out_specs=[pl.BlockSpec((rows, 2 * a_width), rmap(0)), state_spec],
        out_shape=[jax.ShapeDtypeStruct((total_rows, 2 * a_width), _BF16),
                   jax.ShapeDtypeStruct(state_shape, _F32)],
        input_output_aliases=aliases,
        compiler_params=_params(sem),
        name="mixer_even_long" if carry else "mixer_even_short",
    )(*args)


_RING = 3


def _conv_outproj_kernel(z_hbm, bg_hbm, cw_ref, hist_ref, w_ref, x_hbm, o_ref,
                         zbuf_ref, bgbuf_ref, xbuf_ref, sem_ref, zs_ref, h1_ref, h2_ref, wb_ref,
                         *, rows, halo, rows_long, seq_long, seq_short):
    i = pl.program_id(0)
    n_steps = pl.num_programs(0)
    cols = zbuf_ref.shape[2]
    keep = _CONV_TAPS - 1

    def fetch(step, slot):
        src = pl.ds(pl.multiple_of(step * rows, 8), rows)
        return (pltpu.make_async_copy(z_hbm.at[src, :], zbuf_ref.at[slot], sem_ref.at[0, slot]),
                pltpu.make_async_copy(bg_hbm.at[src, :], bgbuf_ref.at[slot], sem_ref.at[1, slot]),
                pltpu.make_async_copy(x_hbm.at[src, :], xbuf_ref.at[slot], sem_ref.at[2, slot]))

    @pl.when(i == 0)
    def _():
        for step in range(_RING - 1):
            for cp in fetch(step, step):
                cp.start()
        wb_ref[...] = w_ref[...].astype(_BF16)
        h1_ref[...] = jnp.zeros(h1_ref.shape, _F32)
        h2_ref[...] = jnp.zeros(h2_ref.shape, _F32)
        zs_ref[0:halo, :] = jnp.zeros((halo, cols), _F32)

    @pl.when(i + _RING - 1 < n_steps)
    def _():
        for cp in fetch(i + _RING - 1, (i + _RING - 1) % _RING):
            cp.start()

    slot = i % _RING
    for cp in fetch(i, slot):
        cp.wait()
    z_ref, bg_ref, x_ref = zbuf_ref.at[slot], bgbuf_ref.at[slot], xbuf_ref.at[slot]

    @pl.when((i + 1) * rows > rows_long)
    def _():
        nh = hist_ref.shape[0]
        r = i * rows + lax.broadcasted_iota(jnp.int32, (rows, nh), 0) - rows_long
        c = lax.broadcasted_iota(jnp.int32, (rows, nh), 1)
        rr = jnp.maximum(r, 0)
        n, t = rr // seq_short, rr % seq_short
        pieces = _split3(hist_ref[...])
        for back, h_ref in ((1, h1_ref), (2, h2_ref)):
            pick = (r >= 0) & (t < back) & (c == keep * n + keep - back + t)
            sel = jnp.where(pick, 1.0, 0.0).astype(_BF16)
            h_ref[...] = _dot(sel, pieces[0]) + _dot(sel, pieces[1]) + _dot(sel, pieces[2])

    zs_ref[halo:halo + rows, :] = z_ref[...].astype(_F32)
    lanes = 128
    row = i * rows + lax.broadcasted_iota(jnp.int32, (rows, lanes), 0)
    t = jnp.where(row >= rows_long, (row - rows_long) % seq_short, row % seq_long)
    has_prev1, has_prev2 = t >= 1, t >= 2

    def gated(sl):
        prev1 = jnp.where(has_prev1, zs_ref[pl.ds(halo - 1, rows), sl], h1_ref[:, sl])
        prev2 = jnp.where(has_prev2, zs_ref[pl.ds(halo - 2, rows), sl], h2_ref[:, sl])
        cur = zs_ref[pl.ds(halo, rows), sl]
        conv = prev2 * cw_ref[0:1, sl] + prev1 * cw_ref[1:2, sl] + cur * cw_ref[2:3, sl]
        return (bg_ref[:, sl].astype(_F32) * conv).astype(_BF16)

    chunk = 2 * lanes
    acc = x_ref[...]
    for kb in range(cols // chunk):
        lo = kb * chunk
        g = jnp.concatenate([gated(slice(lo, lo + lanes)), gated(slice(lo + lanes, lo + chunk))], axis=1)
        acc = acc + _dot(g, wb_ref[lo:lo + chunk, :])
    o_ref[...] = acc
    zs_ref[0:halo, :] = zs_ref[rows:rows + halo, :]


def _conv_outproj_residual(z, bg, cw, hist, w, layer, x, *, rows_long, seq_long, seq_short):
    m, c = z.shape
    n = w.shape[2]
    rows = _ROW_BLOCK // 4
    halo = 16
    assert m % rows == 0 and rows % halo == 0 and _CONV_TAPS == 3
    assert m // rows >= _RING
    in_hbm = pl.BlockSpec(memory_space=pl.ANY)
    return pl.pallas_call(
        functools.partial(_conv_outproj_kernel, rows=rows, halo=halo, rows_long=rows_long,
                          seq_long=seq_long, seq_short=seq_short),
        grid=(m // rows,),
        in_specs=[
            in_hbm,
            in_hbm,
            _layer_spec(cw, layer),
            _layer_spec(hist, layer),
            _resident_weight_spec(w, layer),
            in_hbm,
        ],
        out_specs=pl.BlockSpec((rows, n), lambda i: (i, 0)),
        out_shape=jax.ShapeDtypeStruct((m, n), _F32),
        scratch_shapes=[pltpu.VMEM((_RING, rows, c), _BF16),
                        pltpu.VMEM((_RING, rows, c), _BF16),
                        pltpu.VMEM((_RING, rows, n), _F32),
                        pltpu.SemaphoreType.DMA((3, _RING)),
                        pltpu.VMEM((rows + halo, c), _F32),
                        pltpu.VMEM((rows, c), _F32),
                        pltpu.VMEM((rows, c), _F32),
                        pltpu.VMEM((c, n), _BF16)],
        compiler_params=_params(("arbitrary",), _VMEM_LIMIT_FFN),
        name="conv_outproj_residual",
    )(z, bg, cw, hist, w, x)


def kernel(x_prompt, x_sample, state_gla, state_conv, norm_mix, norm_ffn, norm_final, w_in_even, w_gate_up, b_gate, w_spatial, b_spatial, g_gla_out, w_out_even, w_in_odd, conv_w, w_out_odd, w_ffn_up, w_ffn_down):
    n_p, t_p, d = x_prompt.shape
    n_s, t_s, _ = x_sample.shape
    rows_p, rows_s = n_p * t_p, n_s * t_s
    total = rows_p + rows_s
    depth = norm_mix.shape[0]
    n_even = w_in_even.shape[0]
    a_width = _HEADS * _A_HEAD

    x, x_tail = x_prompt.reshape(rows_p, d), x_sample.reshape(rows_s, d)

    reps = _GLA_ROWS // t_s
    wmix_s = jnp.tile(w_spatial[:, :, :t_s, :t_s], (1, 1, reps, reps))
    bcol_s = jnp.tile(b_spatial[:, :, :t_s], (1, 1, reps))[..., None]
    bcol_p = b_spatial[..., None]
    gain_mix, gain_ffn = norm_mix[:, None, :], norm_ffn[:, None, :]
    gain_final = norm_final[None, None, :]
    gain_gla = g_gla_out[:, None, :]
    bias_gate = b_gate[:, None, :]
    keep = _CONV_TAPS - 1
    hist = state_conv.astype(_F32).reshape(state_conv.shape[0], n_s * keep, d)

    w_in_even_t = jnp.swapaxes(w_in_even, 1, 2).astype(_BF16)
    w_in_odd_b = w_in_odd.astype(_BF16)
    w_gate = w_gate_up.astype(_BF16)

    v_rows, conv_p, conv_s = [], [], []
    gla_p = gla_s = None
    for l in range(depth):
        i = l // 2
        if l % 2 == 0:
            proj, v32, loga = _inproj_even(x, gain_mix, l, w_in_even_t, i, w_gate, bias_gate,
                                           tail_rows=rows_s, x_tail=x_tail)
            mix, gla_p = _mixer_even(
                proj, loga, w_spatial, bcol_p, gain_gla, layer_slot=i, n_slots=n_even,
                total_rows=total, row_start=0, n_seqs=n_p, seq_len=t_p, state_prev=gla_p)
            mix, gla_s = _mixer_even(
                proj, loga, wmix_s, bcol_s, gain_gla, layer_slot=i, n_slots=n_even,
                total_rows=total, row_start=rows_p, n_seqs=n_s, seq_len=t_s,
                state_in=state_gla, mix_prev=mix, state_prev=gla_s)
            v_rows.append(v32.reshape(n_s, t_s, a_width))
            x, x_tail = _outproj_residual(mix, w_out_even, i, x, x_tail), None
        else:
            bg, z, z_b = _inproj_odd(x, gain_mix, l, w_in_odd_b, i)
            conv_p.append(jnp.stack([z[(n + 1) * t_p - keep:(n + 1) * t_p] for n in range(n_p)]))
            conv_s.append(z[rows_p:].reshape(n_s, t_s, d)[:, t_s - keep:])
            x = _conv_outproj_residual(z_b, bg, conv_w, hist, w_out_odd, i, x,
                                       rows_long=rows_p, seq_long=t_p, seq_short=t_s)
        if l < depth - 1:
            x = _ffn_residual(x, gain_ffn, w_ffn_up, w_ffn_down, l)
        else:
            y_p, y_s = _ffn_residual(x, gain_ffn, w_ffn_up, w_ffn_down, l,
                                     final_gain=gain_final, tail_rows=rows_s)

    return (y_p.reshape(n_p, t_p, d), y_s.reshape(n_s, t_s, d), gla_p, gla_s,
            jnp.stack(conv_p), jnp.stack(conv_s), jnp.stack(v_rows))
```

```python
import functools

import jax
import jax.numpy as jnp
from jax import lax
from jax.experimental import pallas as pl
from jax.experimental.pallas import tpu as pltpu

_F32 = jnp.float32
_BF16 = jnp.bfloat16

_EPS = 1e-6
_HEADS = 4
_A_HEAD = 256
_A_CHUNK = 128
_DK = 128
_DV = 256
_GATE_RANK = 16
_GATE_TAU = 16.0
_GLA_ROWS = 64
_CONV_TAPS = 3

_ROW_BLOCK = 1088
_COL_TILE = 512
_MIB = 1024 * 1024
_VMEM_LIMIT = 56 * _MIB
_VMEM_LIMIT_FFN = 62 * _MIB


def _params(semantics, vmem_limit=_VMEM_LIMIT):
    return pltpu.CompilerParams(dimension_semantics=semantics, vmem_limit_bytes=vmem_limit)


def _layer_spec(stacked, layer):
    tail = (0,) * (stacked.ndim - 1)
    return pl.BlockSpec((None,) + stacked.shape[1:], lambda *_: (layer,) + tail)


def _rmsnorm_rows(x, g):
    ms = jnp.mean(x * x, axis=-1, keepdims=True)
    return (x * lax.rsqrt(ms + _EPS)) * g


def _gelu_tanh(x):
    return x * (0.5 * (1.0 + jnp.tanh(0.7978845608028654 * (x + 0.044715 * (x * x * x)))))


def _log_sigmoid(x):
    return jnp.minimum(x, 0.0) - jnp.log(1.0 + jnp.exp(-jnp.abs(x)))


def _dot(a, b):
    return jnp.dot(a, b, preferred_element_type=_F32)


def _dot_nt(a, b):
    return lax.dot_general(a, b, (((1,), (1,)), ((), ())), preferred_element_type=_F32)


def _dot_tn(a, b):
    return lax.dot_general(a, b, (((0,), (0,)), ((), ())), preferred_element_type=_F32)


def _inproj_even_kernel(*refs, gelu_tiles, v_tiles, tail_start, split_rows):
    refs = list(refs)
    x_ref = refs.pop(0)
    xt_ref = refs.pop(0) if split_rows else None
    g_ref, w_ref, wglr_ref, wgate_ref, bgate_ref, proj_ref, v32_ref, loga_ref, hn_ref = refs
    i, j = pl.program_id(0), pl.program_id(1)
    last = pl.num_programs(0) - 1

    def project(hn, activate):
        acc = _dot_nt(hn, w_ref[...])
        proj_ref[...] = (_gelu_tanh(acc) if activate else acc).astype(_BF16)

    if split_rows:
        @pl.when((j == 0) & (i < last))
        def _():
            hn = _rmsnorm_rows(x_ref[...], g_ref[...]).astype(_BF16)
            hn_ref[...] = hn
            project(hn, True)

        @pl.when((j == 0) & (i == last))
        def _():
            hn_ref[:tail_start, :] = _rmsnorm_rows(x_ref[:tail_start, :], g_ref[...]).astype(_BF16)
            hn_ref[tail_start:, :] = _rmsnorm_rows(xt_ref[...], g_ref[...]).astype(_BF16)
            project(hn_ref[...], True)
    else:
        @pl.when(j == 0)
        def _():
            hn = _rmsnorm_rows(x_ref[...], g_ref[...]).astype(_BF16)
            hn_ref[...] = hn
            project(hn, True)

    @pl.when((j > 0) & (j < gelu_tiles))
    def _():
        project(hn_ref[...], True)

    @pl.when(j == gelu_tiles)
    def _():
        hn = hn_ref[...]
        glr = _dot_nt(hn, wglr_ref[...])
        gate = _dot(glr.astype(_BF16), wgate_ref[...]) + bgate_ref[...]
        loga_ref[...] = _log_sigmoid(gate) * (1.0 / _GATE_TAU)
        project(hn, False)

    @pl.when(j > gelu_tiles)
    def _():
        project(hn_ref[...], False)

    @pl.when((i == last) & (j >= v_tiles[0]) & (j < v_tiles[1]))
    def _():
        v32_ref[...] = _gelu_tanh(_dot_nt(hn_ref[tail_start:, :], w_ref[...]))


def _inproj_even(x, g, g_row, w_t, layer, w_gate, b_gate, tail_rows, x_tail=None):
    d = x.shape[1]
    m = x.shape[0] + (0 if x_tail is None else x_tail.shape[0])
    n = w_t.shape[1] - _GATE_RANK
    assert n % _GATE_RANK == 0
    bm, tn = _ROW_BLOCK, 2 * _COL_TILE
    assert tail_rows <= bm and (bm - tail_rows) % 16 == 0
    a_width = _HEADS * _A_HEAD
    gelu_tiles = 2 * a_width // tn
    v_tiles = (a_width // tn, 2 * a_width // tn)
    nk = w_gate.shape[2]

    def v32_map(i, j):
        tile = jnp.clip(j - v_tiles[0], 0, v_tiles[1] - v_tiles[0] - 1)
        return (0, jnp.where(i == m // bm - 1, tile, 0))

    x_specs, x_args = [pl.BlockSpec((bm, d), lambda i, j: (i, 0))], [x]
    if x_tail is not None:
        assert x_tail.shape[0] == tail_rows
        x_specs.append(pl.BlockSpec((tail_rows, d), lambda i, j: (0, 0), pipeline_mode=pl.Buffered(1)))
        x_args.append(x_tail)
    return pl.pallas_call(
        functools.partial(_inproj_even_kernel, gelu_tiles=gelu_tiles, v_tiles=v_tiles,
                          tail_start=bm - tail_rows, split_rows=x_tail is not None),
        grid=(m // bm, n // tn),
        in_specs=x_specs + [
            _layer_spec(g, g_row),
            pl.BlockSpec((None, tn, d), lambda i, j: (layer, j, 0)),
            pl.BlockSpec((None, _GATE_RANK, d), lambda i, j: (layer, n // _GATE_RANK, 0)),
            _layer_spec(w_gate, layer),
            _layer_spec(b_gate, layer),
        ],
        out_specs=[
            pl.BlockSpec((bm, tn), lambda i, j: (i, j)),
            pl.BlockSpec((tail_rows, tn), v32_map),
            pl.BlockSpec((bm, nk), lambda i, j: (i, 0)),
        ],
        out_shape=[
            jax.ShapeDtypeStruct((m, n), _BF16),
            jax.ShapeDtypeStruct((tail_rows, a_width), _F32),
            jax.ShapeDtypeStruct((m, nk), _F32),
        ],
        scratch_shapes=[pltpu.VMEM((bm, d), _BF16)],
        compiler_params=_params(("arbitrary", "arbitrary")),
        name="inproj_even",
    )(*x_args, g, w_t, w_t, w_gate, b_gate)


def _inproj_odd_kernel(x_ref, g_ref, wb_ref, wc_ref, wh_ref, bg_ref, z_ref, zb_ref, hn_ref):
    j = pl.program_id(1)

    def project(hn):
        bg_ref[...] = _dot(hn, wb_ref[...]).astype(_BF16)
        z = _dot(hn, wc_ref[...]) * _dot(hn, wh_ref[...])
        z_ref[...] = z
        zb_ref[...] = z.astype(_BF16)

    @pl.when(j == 0)
    def _():
        hn = _rmsnorm_rows(x_ref[...], g_ref[...]).astype(_BF16)
        hn_ref[...] = hn
        project(hn)

    @pl.when(j > 0)
    def _():
        project(hn_ref[...])


def _inproj_odd(x, g, g_row, w, layer):
    m, d = x.shape
    c = w.shape[2] // 3
    bm, tn = _ROW_BLOCK, _COL_TILE
    nt = c // tn
    return pl.pallas_call(
        _inproj_odd_kernel,
        grid=(m // bm, nt),
        in_specs=[
            pl.BlockSpec((bm, d), lambda i, j: (i, 0)),
            _layer_spec(g, g_row),
            pl.BlockSpec((None, d, tn), lambda i, j: (layer, 0, j)),
            pl.BlockSpec((None, d, tn), lambda i, j: (layer, 0, j + nt)),
            pl.BlockSpec((None, d, tn), lambda i, j: (layer, 0, j + 2 * nt)),
        ],
        out_specs=[pl.BlockSpec((bm, tn), lambda i, j: (i, j))] * 3,
        out_shape=[
            jax.ShapeDtypeStruct((m, c), _BF16),
            jax.ShapeDtypeStruct((m, c), _F32),
            jax.ShapeDtypeStruct((m, c), _BF16),
        ],
        scratch_shapes=[pltpu.VMEM((bm, d), _BF16)],
        compiler_params=_params(("arbitrary", "arbitrary")),
        name="inproj_odd",
    )(x, g, w, w, w)


def _outproj_kernel(*refs, head_rows):
    if head_rows is None:
        a_ref, w_ref, x_ref, o_ref, wb_ref = refs
    else:
        a_ref, w_ref, x_ref, xt_ref, o_ref, wb_ref = refs
    i = pl.program_id(0)

    @pl.when(i == 0)
    def _():
        wb_ref[...] = w_ref[...].astype(_BF16)

    if head_rows is None:
        o_ref[...] = x_ref[...] + _dot(a_ref[...], wb_ref[...])
    else:
        last = pl.num_programs(0) - 1

        @pl.when(i < last)
        def _():
            o_ref[...] = x_ref[...] + _dot(a_ref[...], wb_ref[...])

        @pl.when(i == last)
        def _():
            acc = _dot(a_ref[...], wb_ref[...])
            o_ref[:head_rows, :] = x_ref[:head_rows, :] + acc[:head_rows]
            o_ref[head_rows:, :] = xt_ref[...] + acc[head_rows:]


def _resident_weight_spec(w, layer):
    return pl.BlockSpec((None,) + w.shape[1:], lambda i: (layer, 0, 0), pipeline_mode=pl.Buffered(1))


def _outproj_residual(a, w, layer, x, x_tail=None):
    m, k = a.shape
    n = w.shape[2]
    bm = _ROW_BLOCK // 2
    in_specs = [
        pl.BlockSpec((bm, k), lambda i: (i, 0)),
        _resident_weight_spec(w, layer),
        pl.BlockSpec((bm, n), lambda i: (i, 0)),
    ]
    args = [a, w, x]
    head_rows = None
    if x_tail is not None:
        head_rows = x.shape[0] - (m // bm - 1) * bm
        assert x.shape[0] + x_tail.shape[0] == m and head_rows + x_tail.shape[0] == bm
        assert head_rows % 8 == 0 and head_rows > 0
        in_specs.append(pl.BlockSpec(x_tail.shape, lambda i: (0, 0), pipeline_mode=pl.Buffered(1)))
        args.append(x_tail)
    return pl.pallas_call(
        functools.partial(_outproj_kernel, head_rows=head_rows),
        grid=(m // bm,),
        in_specs=in_specs,
        out_specs=pl.BlockSpec((bm, n), lambda i: (i, 0)),
        out_shape=jax.ShapeDtypeStruct((m, n), _F32),
        scratch_shapes=[pltpu.VMEM((k, n), _BF16)],
        compiler_params=_params(("arbitrary",)),
        name="outproj_residual",
    )(*args)


def _ffn_kernel(x_ref, g_ref, wup_ref, wdn_ref, *rest, tail_start):
    final = len(rest) == 4
    if final:
        gfin_ref, o_ref, tail_ref, hn_ref = rest
    else:
        o_ref, hn_ref = rest
    i, j = pl.program_id(0), pl.program_id(1)

    def mlp_tile(hn):
        h = _dot(hn, wup_ref[...].astype(_BF16))
        h = jnp.square(jnp.maximum(h, 0.0)).astype(_BF16)
        return _dot(h, wdn_ref[...].astype(_BF16))

    @pl.when(j == 0)
    def _():
        x = x_ref[...]
        hn = _rmsnorm_rows(x, g_ref[...]).astype(_BF16)
        hn_ref[...] = hn
        o_ref[...] = x + mlp_tile(hn)

    @pl.when(j > 0)
    def _():
        o_ref[...] += mlp_tile(hn_ref[...])

    if final:
        @pl.when(j == pl.num_programs(1) - 1)
        def _():
            o_ref[...] = _rmsnorm_rows(o_ref[...], gfin_ref[...])

            @pl.when(i == pl.num_programs(0) - 1)
            def _():
                tail_ref[...] = o_ref[tail_start:, :]


def _ffn_residual(x, g, w_up, w_down, layer, final_gain=None, tail_rows=0):
    m, d = x.shape
    f = w_up.shape[2]
    bm, tf = _ROW_BLOCK, _COL_TILE
    in_specs = [
        pl.BlockSpec((bm, d), lambda i, j: (i, 0)),
        _layer_spec(g, layer),
        pl.BlockSpec((None, d, tf), lambda i, j: (layer, 0, j)),
        pl.BlockSpec((None, tf, d), lambda i, j: (layer, j, 0)),
    ]
    args = [x, g, w_up, w_down]
    row_block = pl.BlockSpec((bm, d), lambda i, j: (i, 0))
    if final_gain is None:
        out_specs = row_block
        out_shape = jax.ShapeDtypeStruct((m, d), _F32)
    else:
        assert tail_rows <= bm and (bm - tail_rows) % 8 == 0
        in_specs.append(_layer_spec(final_gain, 0))
        args.append(final_gain)
        out_specs = [row_block, pl.BlockSpec((tail_rows, d), lambda i, j: (0, 0))]
        out_shape = [jax.ShapeDtypeStruct((m - tail_rows, d), _F32),
                     jax.ShapeDtypeStruct((tail_rows, d), _F32)]
    return pl.pallas_call(
        functools.partial(_ffn_kernel, tail_start=bm - tail_rows),
        grid=(m // bm, f // tf),
        in_specs=in_specs,
        out_specs=out_specs,
        out_shape=out_shape,
        scratch_shapes=[pltpu.VMEM((bm, d), _BF16)],
        compiler_params=_params(("arbitrary", "arbitrary"), _VMEM_LIMIT_FFN),
        name="ffn_residual" if final_gain is None else "ffn_final",
    )(*args)


def _split3(x):
    hi = x.astype(_BF16)
    r1 = x - hi.astype(_F32)
    mid = r1.astype(_BF16)
    lo = (r1 - mid.astype(_F32)).astype(_BF16)
    return hi, mid, lo


def _mixer_even_kernel(*refs, rows, a_chunk, a_seq, seqs_per_group, carry_state):
    (u_ref, v_ref, q_ref, k_ref, vb_ref, r_ref, la_ref, wmix_ref, bcol_ref, gout_ref) = refs[:10]
    rest = refs[10:]
    if carry_state:
        mix_ref, sout_ref = rest[-2:]
        sin_ref = None
    else:
        sin_ref = rest[0]
        mix_ref, sout_ref = rest[-2:]

    tt = lax.broadcasted_iota(jnp.int32, (a_chunk, a_chunk), 0)
    ss = lax.broadcasted_iota(jnp.int32, (a_chunk, a_chunk), 1)
    a_mask = (ss <= tt) & ((tt // a_seq) == (ss // a_seq))
    for h in range(_HEADS):
        cols = slice(h * _A_HEAD, (h + 1) * _A_HEAD)
        wm = jnp.where(a_mask, wmix_ref[h], 0.0).astype(_BF16)
        bias = bcol_ref[h]
        for c in range(rows // a_chunk):
            rws = slice(c * a_chunk, (c + 1) * a_chunk)
            mixed = _dot(wm, v_ref[rws, cols]) + bias
            mix_ref[rws, cols] = (u_ref[rws, cols].astype(_F32) * mixed).astype(_BF16)

    gr = _GLA_ROWS
    seq_rows = gr // seqs_per_group
    ti = lax.broadcasted_iota(jnp.int32, (gr, gr), 0)
    si = lax.broadcasted_iota(jnp.int32, (gr, gr), 1)
    same_seq = (ti // seq_rows) == (si // seq_rows)
    causal = (si <= ti) & same_seq
    cum_and_total = jnp.concatenate(
        [jnp.where(causal, 1.0, 0.0), jnp.where(same_seq, 1.0, 0.0)], axis=0).astype(_BF16)
    kw = _HEADS * _DK
    row_k = lax.broadcasted_iota(jnp.int32, (gr, _DK), 0)
    row_v = lax.broadcasted_iota(jnp.int32, (gr, _DV), 0)
    pos_all = lax.broadcasted_iota(jnp.int32, (gr, kw), 0) % seq_rows
    ones_v = jnp.ones((gr, _DV), _BF16)
    scale = _DK ** -0.5
    a_width = _HEADS * _A_HEAD

    if carry_state:
        @pl.when(pl.program_id(1) == 0)
        def _():
            sout_ref[...] = jnp.zeros(sout_ref.shape, _F32)
        states = [sout_ref[h] for h in range(_HEADS)]

    for gi in range(rows // gr):
        rws = slice(gi * gr, (gi + 1) * gr)
        la = la_ref[rws, :]
        la_hi = la.astype(_BF16)
        la_lo = (la - la_hi.astype(_F32)).astype(_BF16)
        sums = _dot(cum_and_total, jnp.concatenate([la_hi, la_lo], axis=1))
        g_cum = sums[:gr, :kw] + sums[:gr, kw:]
        g_tot = sums[gr:, :kw] + sums[gr:, kw:]
        q = q_ref[rws, :].astype(_F32) * scale
        k = k_ref[rws, :].astype(_F32)
        q_in_all = (q * jnp.exp(g_cum)).astype(_BF16)
        k_in_all = (k * jnp.exp(-g_cum)).astype(_BF16)
        k_dec_all = (k * jnp.exp(g_tot - g_cum)).astype(_BF16)
        t_hi, t_mid, t_lo = _split3(g_tot)
        pieces_all = jnp.where(pos_all == 0, t_hi,
                               jnp.where(pos_all == 1, t_mid,
                                         jnp.where(pos_all == 2, t_lo, jnp.zeros_like(t_hi))))
        for h in range(_HEADS):
            kc = slice(h * _DK, (h + 1) * _DK)
            vc = slice(h * _DV, (h + 1) * _DV)
            q_in, k_in, k_dec, pieces = q_in_all[:, kc], k_in_all[:, kc], k_dec_all[:, kc], pieces_all[:, kc]
            v = vb_ref[rws, vc]
            scores = jnp.where(causal, _dot_nt(q_in, k_in), 0.0).astype(_BF16)
            if carry_state:
                o = _dot(jnp.concatenate([q_in, scores], axis=1),
                         jnp.concatenate([states[h].astype(_BF16), v], axis=0))
                decay = jnp.exp(_dot_tn(pieces, ones_v))
                states[h] = states[h] * decay + _dot_tn(k_dec, v)
            else:
                o = _dot(scores, v)
                zero = jnp.zeros_like(pieces)
                for s in range(seqs_per_group):
                    in_seq_k = (row_k // seq_rows) == s
                    in_seq_v = (row_v // seq_rows) == s
                    s0 = sin_ref[s, h]
                    o = o + jnp.where(in_seq_v, _dot(q_in, s0.astype(_BF16)), 0.0)
                    decay = jnp.exp(_dot_tn(jnp.where(in_seq_k, pieces, zero), ones_v))
                    kd = jnp.where(in_seq_k, k_dec, zero)
                    sout_ref[s, h] = s0 * decay + _dot_tn(kd, v)
            ms = jnp.mean(o * o, axis=-1, keepdims=True)
            on = (o * lax.rsqrt(ms + _EPS)) * gout_ref[:, vc]
            r = r_ref[rws, vc].astype(_F32)
            gated = on * (r * (1.0 / (1.0 + jnp.exp(-r))))
            mix_ref[rws, a_width + h * _DV:a_width + (h + 1) * _DV] = gated.astype(_BF16)

    if carry_state:
        for h in range(_HEADS):
            sout_ref[h] = states[h]


def _mixer_even(proj, loga, wmix, bcol, gout, *, layer_slot, n_slots, total_rows,
                row_start, n_seqs, seq_len, state_in=None, mix_prev=None, state_prev=None):
    carry = state_in is None
    a_width = _HEADS * _A_HEAD
    kw = _HEADS * _DK
    if carry:
        rows = 512
        blocks_per_seq = seq_len // rows
        grid = (n_seqs, blocks_per_seq)
        first = row_start // rows
        rmap = lambda col: (lambda n, c: (first + n * blocks_per_seq + c, col))
        a_chunk, a_seq, spg = _A_CHUNK, _A_CHUNK, 1
        state_shape = (n_slots, n_seqs, _HEADS, _DK, _DV)
        state_spec = pl.BlockSpec((None, None, _HEADS, _DK, _DV),
                                  lambda n, c: (layer_slot, n, 0, 0, 0))
        sem = ("arbitrary", "arbitrary")
    else:
        rows = _GLA_ROWS
        spg = rows // seq_len
        grid = (n_seqs // spg,)
        first = row_start // rows
        rmap = lambda col: (lambda g: (first + g, col))
        a_chunk, a_seq = rows, seq_len
        state_shape = (n_slots, n_seqs, _HEADS, _DK, _DV)
        state_spec = pl.BlockSpec((None, spg, _HEADS, _DK, _DV),
                                  lambda g: (layer_slot, g, 0, 0, 0))
        sem = ("arbitrary",)
    in_specs = [
        pl.BlockSpec((rows, a_width), rmap(0)),
        pl.BlockSpec((rows, a_width), rmap(1)),
        pl.BlockSpec((rows, kw), rmap(2 * a_width // kw)),
        pl.BlockSpec((rows, kw), rmap(2 * a_width // kw + 1)),
        pl.BlockSpec((rows, a_width), rmap(3)),
        pl.BlockSpec((rows, a_width), rmap(4)),
        pl.BlockSpec((rows, kw), rmap(0)),
        _layer_spec(wmix, layer_slot),
        _layer_spec(bcol, layer_slot),
        _layer_spec(gout, layer_slot),
    ]
    args = [proj, proj, proj, proj, proj, proj, loga, wmix, bcol, gout]
    if not carry:
        in_specs.append(state_spec)
        args.append(state_in)
    aliases = {}
    if mix_prev is not None:
        aliases[len(args)] = 0
        in_specs.append(pl.BlockSpec(memory_space=pl.ANY))
        args.append(mix_prev)
    if state_prev is not None:
        aliases[len(args)] = 1
        in_specs.append(pl.BlockSpec(memory_space=pl.ANY))
        args.append(state_prev)
    return pl.pallas_call(
        functools.partial(_mixer_even_kernel, rows=rows, a_chunk=a_chunk, a_seq=a_seq,
                          seqs_per_group=spg, carry_state=carry),
        grid=grid,
        in_specs=in_specs,
        out_specs=[pl.BlockSpec((rows, 2 * a_width), rmap(0)), state_spec],
        out_shape=[jax.ShapeDtypeStruct((total_rows, 2 * a_width), _BF16),
                   jax.ShapeDtypeStruct(state_shape, _F32)],
        input_output_aliases=aliases,
        compiler_params=_params(sem),
        name="mixer_even_long" if carry else "mixer_even_short",
    )(*args)


_RING = 3


def _conv_outproj_kernel(z_hbm, bg_hbm, cw_ref, hist_ref, w_ref, x_hbm, o_ref,
                         zbuf_ref, bgbuf_ref, xbuf_ref, sem_ref, zs_ref, h1_ref, h2_ref, wb_ref,
                         *, rows, halo, rows_long, seq_long, seq_short):
    i = pl.program_id(0)
    n_steps = pl.num_programs(0)
    cols = zbuf_ref.shape[2]
    keep = _CONV_TAPS - 1

    def fetch(step, slot):
        src = pl.ds(pl.multiple_of(step * rows, 8), rows)
        return (pltpu.make_async_copy(z_hbm.at[src, :], zbuf_ref.at[slot], sem_ref.at[0, slot]),
                pltpu.make_async_copy(bg_hbm.at[src, :], bgbuf_ref.at[slot], sem_ref.at[1, slot]),
                pltpu.make_async_copy(x_hbm.at[src, :], xbuf_ref.at[slot], sem_ref.at[2, slot]))

    @pl.when(i == 0)
    def _():
        for step in range(_RING - 1):
            for cp in fetch(step, step):
                cp.start()
        wb_ref[...] = w_ref[...].astype(_BF16)
        h1_ref[...] = jnp.zeros(h1_ref.shape, _F32)
        h2_ref[...] = jnp.zeros(h2_ref.shape, _F32)
        zs_ref[0:halo, :] = jnp.zeros((halo, cols), _F32)

    @pl.when(i + _RING - 1 < n_steps)
    def _():
        for cp in fetch(i + _RING - 1, (i + _RING - 1) % _RING):
            cp.start()

    slot = i % _RING
    for cp in fetch(i, slot):
        cp.wait()
    z_ref, bg_ref, x_ref = zbuf_ref.at[slot], bgbuf_ref.at[slot], xbuf_ref.at[slot]

    @pl.when((i + 1) * rows > rows_long)
    def _():
        nh = hist_ref.shape[0]
        r = i * rows + lax.broadcasted_iota(jnp.int32, (rows, nh), 0) - rows_long
        c = lax.broadcasted_iota(jnp.int32, (rows, nh), 1)
        rr = jnp.maximum(r, 0)
        n, t = rr // seq_short, rr % seq_short
        pieces = _split3(hist_ref[...])
        for back, h_ref in ((1, h1_ref), (2, h2_ref)):
            pick = (r >= 0) & (t < back) & (c == keep * n + keep - back + t)
            sel = jnp.where(pick, 1.0, 0.0).astype(_BF16)
            h_ref[...] = _dot(sel, pieces[0]) + _dot(sel, pieces[1]) + _dot(sel, pieces[2])

    zs_ref[halo:halo + rows, :] = z_ref[...].astype(_F32)
    lanes = 128
    row = i * rows + lax.broadcasted_iota(jnp.int32, (rows, lanes), 0)
    t = jnp.where(row >= rows_long, (row - rows_long) % seq_short, row % seq_long)
    has_prev1, has_prev2 = t >= 1, t >= 2

    def gated(sl):
        prev1 = jnp.where(has_prev1, zs_ref[pl.ds(halo - 1, rows), sl], h1_ref[:, sl])
        prev2 = jnp.where(has_prev2, zs_ref[pl.ds(halo - 2, rows), sl], h2_ref[:, sl])
        cur = zs_ref[pl.ds(halo, rows), sl]
        conv = prev2 * cw_ref[0:1, sl] + prev1 * cw_ref[1:2, sl] + cur * cw_ref[2:3, sl]
        return (bg_ref[:, sl].astype(_F32) * conv).astype(_BF16)

    chunk = 2 * lanes
    acc = x_ref[...]
    for kb in range(cols // chunk):
        lo = kb * chunk
        g = jnp.concatenate([gated(slice(lo, lo + lanes)), gated(slice(lo + lanes, lo + chunk))], axis=1)
        acc = acc + _dot(g, wb_ref[lo:lo + chunk, :])
    o_ref[...] = acc
    zs_ref[0:halo, :] = zs_ref[rows:rows + halo, :]


def _conv_outproj_residual(z, bg, cw, hist, w, layer, x, *, rows_long, seq_long, seq_short):
    m, c = z.shape
    n = w.shape[2]
    rows = _ROW_BLOCK // 4
    halo = 16
    assert m % rows == 0 and rows % halo == 0 and _CONV_TAPS == 3
    assert m // rows >= _RING
    in_hbm = pl.BlockSpec(memory_space=pl.ANY)
    return pl.pallas_call(
        functools.partial(_conv_outproj_kernel, rows=rows, halo=halo, rows_long=rows_long,
                          seq_long=seq_long, seq_short=seq_short),
        grid=(m // rows,),
        in_specs=[
            in_hbm,
            in_hbm,
            _layer_spec(cw, layer),
            _layer_spec(hist, layer),
            _resident_weight_spec(w, layer),
            in_hbm,
        ],
        out_specs=pl.BlockSpec((rows, n), lambda i: (i, 0)),
        out_shape=jax.ShapeDtypeStruct((m, n), _F32),
        scratch_shapes=[pltpu.VMEM((_RING, rows, c), _BF16),
                        pltpu.VMEM((_RING, rows, c), _BF16),
                        pltpu.VMEM((_RING, rows, n), _F32),
                        pltpu.SemaphoreType.DMA((3, _RING)),
                        pltpu.VMEM((rows + halo, c), _F32),
                        pltpu.VMEM((rows, c), _F32),
                        pltpu.VMEM((rows, c), _F32),
                        pltpu.VMEM((c, n), _BF16)],
        compiler_params=_params(("arbitrary",), _VMEM_LIMIT_FFN),
        name="conv_outproj_residual",
    )(z, bg, cw, hist, w, x)


def kernel(x_prompt, x_sample, state_gla, state_conv, norm_mix, norm_ffn, norm_final, w_in_even, w_gate_up, b_gate, w_spatial, b_spatial, g_gla_out, w_out_even, w_in_odd, conv_w, w_out_odd, w_ffn_up, w_ffn_down):
    n_p, t_p, d = x_prompt.shape
    n_s, t_s, _ = x_sample.shape
    rows_p, rows_s = n_p * t_p, n_s * t_s
    total = rows_p + rows_s
    depth = norm_mix.shape[0]
    n_even = w_in_even.shape[0]
    a_width = _HEADS * _A_HEAD

    x, x_tail = x_prompt.reshape(rows_p, d), x_sample.reshape(rows_s, d)

    reps = _GLA_ROWS // t_s
    wmix_s = jnp.tile(w_spatial[:, :, :t_s, :t_s], (1, 1, reps, reps))
    bcol_s = jnp.tile(b_spatial[:, :, :t_s], (1, 1, reps))[..., None]
    bcol_p = b_spatial[..., None]
    gain_mix, gain_ffn = norm_mix[:, None, :], norm_ffn[:, None, :]
    gain_final = norm_final[None, None, :]
    gain_gla = g_gla_out[:, None, :]
    bias_gate = b_gate[:, None, :]
    keep = _CONV_TAPS - 1
    hist = state_conv.astype(_F32).reshape(state_conv.shape[0], n_s * keep, d)

    w_in_even_t = jnp.swapaxes(w_in_even, 1, 2).astype(_BF16)
    w_in_odd_b = w_in_odd.astype(_BF16)
    w_gate = w_gate_up.astype(_BF16)

    v_rows, conv_p, conv_s = [], [], []
    gla_p = gla_s = None
    for l in range(depth):
        i = l // 2
        if l % 2 == 0:
            proj, v32, loga = _inproj_even(x, gain_mix, l, w_in_even_t, i, w_gate, bias_gate,
                                           tail_rows=rows_s, x_tail=x_tail)
            mix, gla_p = _mixer_even(
                proj, loga, w_spatial, bcol_p, gain_gla, layer_slot=i, n_slots=n_even,
                total_rows=total, row_start=0, n_seqs=n_p, seq_len=t_p, state_prev=gla_p)
            mix, gla_s = _mixer_even(
                proj, loga, wmix_s, bcol_s, gain_gla, layer_slot=i, n_slots=n_even,
                total_rows=total, row_start=rows_p, n_seqs=n_s, seq_len=t_s,
                state_in=state_gla, mix_prev=mix, state_prev=gla_s)
            v_rows.append(v32.reshape(n_s, t_s, a_width))
            x, x_tail = _outproj_residual(mix, w_out_even, i, x, x_tail), None
        else:
            bg, z, z_b = _inproj_odd(x, gain_mix, l, w_in_odd_b, i)
            conv_p.append(jnp.stack([z[(n + 1) * t_p - keep:(n + 1) * t_p] for n in range(n_p)]))
            conv_s.append(z[rows_p:].reshape(n_s, t_s, d)[:, t_s - keep:])
            x = _conv_outproj_residual(z_b, bg, conv_w, hist, w_out_odd, i, x,
                                       rows_long=rows_p, seq_long=t_p, seq_short=t_s)
        if l < depth - 1:
            x = _ffn_residual(x, gain_ffn, w_ffn_up, w_ffn_down, l)
        else:
            y_p, y_s = _ffn_residual(x, gain_ffn, w_ffn_up, w_ffn_down, l,
                                     final_gain=gain_final, tail_rows=rows_s)

    return (y_p.reshape(n_p, t_p, d), y_s.reshape(n_s, t_s, d), gla_p, gla_s,
            jnp.stack(conv_p), jnp.stack(conv_s), jnp.stack(v_rows))
```

```python
import functools

import jax
import jax.numpy as jnp
from jax import lax
from jax.experimental import pallas as pl
from jax.experimental.pallas import tpu as pltpu

_F32 = jnp.float32
_BF16 = jnp.bfloat16

_EPS = 1e-6
_HEADS = 4
_A_HEAD = 256
_A_CHUNK = 128
_DK = 128
_DV = 256
_GATE_RANK = 16
_GATE_TAU = 16.0
_GLA_ROWS = 64
_CONV_TAPS = 3

_ROW_BLOCK = 1088
_COL_TILE = 512
_MIB = 1024 * 1024
_VMEM_LIMIT = 56 * _MIB
_VMEM_LIMIT_FFN = 62 * _MIB


def _params(semantics, vmem_limit=_VMEM_LIMIT):
    return pltpu.CompilerParams(dimension_semantics=semantics, vmem_limit_bytes=vmem_limit)


def _layer_spec(stacked, layer):
    tail = (0,) * (stacked.ndim - 1)
    return pl.BlockSpec((None,) + stacked.shape[1:], lambda *_: (layer,) + tail)


def _rmsnorm_rows(x, g):
    ms = jnp.mean(x * x, axis=-1, keepdims=True)
    return (x * lax.rsqrt(ms + _EPS)) * g


def _gelu_tanh(x):
    return x * (0.5 * (1.0 + jnp.tanh(0.7978845608028654 * (x + 0.044715 * (x * x * x)))))


def _log_sigmoid(x):
    return jnp.minimum(x, 0.0) - jnp.log(1.0 + jnp.exp(-jnp.abs(x)))


def _dot(a, b):
    return jnp.dot(a, b, preferred_element_type=_F32)


def _dot_nt(a, b):
    return lax.dot_general(a, b, (((1,), (1,)), ((), ())), preferred_element_type=_F32)


def _dot_tn(a, b):
    return lax.dot_general(a, b, (((0,), (0,)), ((), ())), preferred_element_type=_F32)


def _inproj_even_kernel(*refs, gelu_tiles, v_tiles, tail_start, split_rows):
    refs = list(refs)
    x_ref = refs.pop(0)
    xt_ref = refs.pop(0) if split_rows else None
    g_ref, w_ref, wglr_ref, wgate_ref, bgate_ref, proj_ref, v32_ref, loga_ref, hn_ref = refs
    i, j = pl.program_id(0), pl.program_id(1)
    last = pl.num_programs(0) - 1

    def project(hn, activate):
        acc = _dot_nt(hn, w_ref[...])
        proj_ref[...] = (_gelu_tanh(acc) if activate else acc).astype(_BF16)

    if split_rows:
        @pl.when((j == 0) & (i < last))
        def _():
            hn = _rmsnorm_rows(x_ref[...], g_ref[...]).astype(_BF16)
            hn_ref[...] = hn
            project(hn, True)

        @pl.when((j == 0) & (i == last))
        def _():
            hn_ref[:tail_start, :] = _rmsnorm_rows(x_ref[:tail_start, :], g_ref[...]).astype(_BF16)
            hn_ref[tail_start:, :] = _rmsnorm_rows(xt_ref[...], g_ref[...]).astype(_BF16)
            project(hn_ref[...], True)
    else:
        @pl.when(j == 0)
        def _():
            hn = _rmsnorm_rows(x_ref[...], g_ref[...]).astype(_BF16)
            hn_ref[...] = hn
            project(hn, True)

    @pl.when((j > 0) & (j < gelu_tiles))
    def _():
        project(hn_ref[...], True)

    @pl.when(j == gelu_tiles)
    def _():
        hn = hn_ref[...]
        glr = _dot_nt(hn, wglr_ref[...])
        gate = _dot(glr.astype(_BF16), wgate_ref[...]) + bgate_ref[...]
        loga_ref[...] = _log_sigmoid(gate) * (1.0 / _GATE_TAU)
        project(hn, False)

    @pl.when(j > gelu_tiles)
    def _():
        project(hn_ref[...], False)

    @pl.when((i == last) & (j >= v_tiles[0]) & (j < v_tiles[1]))
    def _():
        v32_ref[...] = _gelu_tanh(_dot_nt(hn_ref[tail_start:, :], w_ref[...]))


def _inproj_even(x, g, g_row, w_t, layer, w_gate, b_gate, tail_rows, x_tail=None):
    d = x.shape[1]
    m = x.shape[0] + (0 if x_tail is None else x_tail.shape[0])
    n = w_t.shape[1] - _GATE_RANK
    assert n % _GATE_RANK == 0
    bm, tn = _ROW_BLOCK, 2 * _COL_TILE
    assert tail_rows <= bm and (bm - tail_rows) % 16 == 0
    a_width = _HEADS * _A_HEAD
    gelu_tiles = 2 * a_width // tn
    v_tiles = (a_width // tn, 2 * a_width // tn)
    nk = w_gate.shape[2]

    def v32_map(i, j):
        tile = jnp.clip(j - v_tiles[0], 0, v_tiles[1] - v_tiles[0] - 1)
        return (0, jnp.where(i == m // bm - 1, tile, 0))

    x_specs, x_args = [pl.BlockSpec((bm, d), lambda i, j: (i, 0))], [x]
    if x_tail is not None:
        assert x_tail.shape[0] == tail_rows
        x_specs.append(pl.BlockSpec((tail_rows, d), lambda i, j: (0, 0), pipeline_mode=pl.Buffered(1)))
        x_args.append(x_tail)
    return pl.pallas_call(
        functools.partial(_inproj_even_kernel, gelu_tiles=gelu_tiles, v_tiles=v_tiles,
                          tail_start=bm - tail_rows, split_rows=x_tail is not None),
        grid=(m // bm, n // tn),
        in_specs=x_specs + [
            _layer_spec(g, g_row),
            pl.BlockSpec((None, tn, d), lambda i, j: (layer, j, 0)),
            pl.BlockSpec((None, _GATE_RANK, d), lambda i, j: (layer, n // _GATE_RANK, 0)),
            _layer_spec(w_gate, layer),
            _layer_spec(b_gate, layer),
        ],
        out_specs=[
            pl.BlockSpec((bm, tn), lambda i, j: (i, j)),
            pl.BlockSpec((tail_rows, tn), v32_map),
            pl.BlockSpec((bm, nk), lambda i, j: (i, 0)),
        ],
        out_shape=[
            jax.ShapeDtypeStruct((m, n), _BF16),
            jax.ShapeDtypeStruct((tail_rows, a_width), _F32),
            jax.ShapeDtypeStruct((m, nk), _F32),
        ],
        scratch_shapes=[pltpu.VMEM((bm, d), _BF16)],
        compiler_params=_params(("arbitrary", "arbitrary")),
        name="inproj_even",
    )(*x_args, g, w_t, w_t, w_gate, b_gate)


def _inproj_odd_kernel(x_ref, g_ref, wb_ref, wc_ref, wh_ref, bg_ref, z_ref, zb_ref, hn_ref):
    j = pl.program_id(1)

    def project(hn):
        bg_ref[...] = _dot(hn, wb_ref[...]).astype(_BF16)
        z = _dot(hn, wc_ref[...]) * _dot(hn, wh_ref[...])
        z_ref[...] = z
        zb_ref[...] = z.astype(_BF16)

    @pl.when(j == 0)
    def _():
        hn = _rmsnorm_rows(x_ref[...], g_ref[...]).astype(_BF16)
        hn_ref[...] = hn
        project(hn)

    @pl.when(j > 0)
    def _():
        project(hn_ref[...])


def _inproj_odd(x, g, g_row, w, layer):
    m, d = x.shape
    c = w.shape[2] // 3
    bm, tn = _ROW_BLOCK, _COL_TILE
    nt = c // tn
    return pl.pallas_call(
        _inproj_odd_kernel,
        grid=(m // bm, nt),
        in_specs=[
            pl.BlockSpec((bm, d), lambda i, j: (i, 0)),
            _layer_spec(g, g_row),
            pl.BlockSpec((None, d, tn), lambda i, j: (layer, 0, j)),
            pl.BlockSpec((None, d, tn), lambda i, j: (layer, 0, j + nt)),
            pl.BlockSpec((None, d, tn), lambda i, j: (layer, 0, j + 2 * nt)),
        ],
        out_specs=[pl.BlockSpec((bm, tn), lambda i, j: (i, j))] * 3,
        out_shape=[
            jax.ShapeDtypeStruct((m, c), _BF16),
            jax.ShapeDtypeStruct((m, c), _F32),
            jax.ShapeDtypeStruct((m, c), _BF16),
        ],
        scratch_shapes=[pltpu.VMEM((bm, d), _BF16)],
        compiler_params=_params(("arbitrary", "arbitrary")),
        name="inproj_odd",
    )(x, g, w, w, w)


def _outproj_kernel(*refs, head_rows):
    if head_rows is None:
        a_ref, w_ref, x_ref, o_ref, wb_ref = refs
    else:
        a_ref, w_ref, x_ref, xt_ref, o_ref, wb_ref = refs
    i = pl.program_id(0)

    @pl.when(i == 0)
    def _():
        wb_ref[...] = w_ref[...].astype(_BF16)

    if head_rows is None:
        o_ref[...] = x_ref[...] + _dot(a_ref[...], wb_ref[...])
    else:
        last = pl.num_programs(0) - 1

        @pl.when(i < last)
        def _():
            o_ref[...] = x_ref[...] + _dot(a_ref[...], wb_ref[...])

        @pl.when(i == last)
        def _():
            acc = _dot(a_ref[...], wb_ref[...])
            o_ref[:head_rows, :] = x_ref[:head_rows, :] + acc[:head_rows]
            o_ref[head_rows:, :] = xt_ref[...] + acc[head_rows:]


def _resident_weight_spec(w, layer):
    return pl.BlockSpec((None,) + w.shape[1:], lambda i: (layer, 0, 0), pipeline_mode=pl.Buffered(1))


def _outproj_residual(a, w, layer, x, x_tail=None):
    m, k = a.shape
    n = w.shape[2]
    bm = _ROW_BLOCK // 2
    in_specs = [
        pl.BlockSpec((bm, k), lambda i: (i, 0)),
        _resident_weight_spec(w, layer),
        pl.BlockSpec((bm, n), lambda i: (i, 0)),
    ]
    args = [a, w, x]
    head_rows = None
    if x_tail is not None:
        head_rows = x.shape[0] - (m // bm - 1) * bm
        assert x.shape[0] + x_tail.shape[0] == m and head_rows + x_tail.shape[0] == bm
        assert head_rows % 8 == 0 and head_rows > 0
        in_specs.append(pl.BlockSpec(x_tail.shape, lambda i: (0, 0), pipeline_mode=pl.Buffered(1)))
        args.append(x_tail)
    return pl.pallas_call(
        functools.partial(_outproj_kernel, head_rows=head_rows),
        grid=(m // bm,),
        in_specs=in_specs,
        out_specs=pl.BlockSpec((bm, n), lambda i: (i, 0)),
        out_shape=jax.ShapeDtypeStruct((m, n), _F32),
        scratch_shapes=[pltpu.VMEM((k, n), _BF16)],
        compiler_params=_params(("arbitrary",)),
        name="outproj_residual",
    )(*args)


def _ffn_kernel(x_ref, g_ref, wup_ref, wdn_ref, *rest, tail_start):
    final = len(rest) == 4
    if final:
        gfin_ref, o_ref, tail_ref, hn_ref = rest
    else:
        o_ref, hn_ref = rest
    i, j = pl.program_id(0), pl.program_id(1)

    def mlp_tile(hn):
        h = _dot(hn, wup_ref[...].astype(_BF16))
        h = jnp.square(jnp.maximum(h, 0.0)).astype(_BF16)
        return _dot(h, wdn_ref[...].astype(_BF16))

    @pl.when(j == 0)
    def _():
        x = x_ref[...]
        hn = _rmsnorm_rows(x, g_ref[...]).astype(_BF16)
        hn_ref[...] = hn
        o_ref[...] = x + mlp_tile(hn)

    @pl.when(j > 0)
    def _():
        o_ref[...] += mlp_tile(hn_ref[...])

    if final:
        @pl.when(j == pl.num_programs(1) - 1)
        def _():
            o_ref[...] = _rmsnorm_rows(o_ref[...], gfin_ref[...])

            @pl.when(i == pl.num_programs(0) - 1)
            def _():
                tail_ref[...] = o_ref[tail_start:, :]


def _ffn_residual(x, g, w_up, w_down, layer, final_gain=None, tail_rows=0):
    m, d = x.shape
    f = w_up.shape[2]
    bm, tf = _ROW_BLOCK, _COL_TILE
    in_specs = [
        pl.BlockSpec((bm, d), lambda i, j: (i, 0)),
        _layer_spec(g, layer),
        pl.BlockSpec((None, d, tf), lambda i, j: (layer, 0, j)),
        pl.BlockSpec((None, tf, d), lambda i, j: (layer, j, 0)),
    ]
    args = [x, g, w_up, w_down]
    row_block = pl.BlockSpec((bm, d), lambda i, j: (i, 0))
    if final_gain is None:
        out_specs = row_block
        out_shape = jax.ShapeDtypeStruct((m, d), _F32)
    else:
        assert tail_rows <= bm and (bm - tail_rows) % 8 == 0
        in_specs.append(_layer_spec(final_gain, 0))
        args.append(final_gain)
        out_specs = [row_block, pl.BlockSpec((tail_rows, d), lambda i, j: (0, 0))]
        out_shape = [jax.ShapeDtypeStruct((m - tail_rows, d), _F32),
                     jax.ShapeDtypeStruct((tail_rows, d), _F32)]
    return pl.pallas_call(
        functools.partial(_ffn_kernel, tail_start=bm - tail_rows),
        grid=(m // bm, f // tf),
        in_specs=in_specs,
        out_specs=out_specs,
        out_shape=out_shape,
        scratch_shapes=[pltpu.VMEM((bm, d), _BF16)],
        compiler_params=_params(("arbitrary", "arbitrary"), _VMEM_LIMIT_FFN),
        name="ffn_residual" if final_gain is None else "ffn_final",
    )(*args)


def _split3(x):
    hi = x.astype(_BF16)
    r1 = x - hi.astype(_F32)
    mid = r1.astype(_BF16)
    lo = (r1 - mid.astype(_F32)).astype(_BF16)
    return hi, mid, lo


def _mixer_even_kernel(*refs, rows, a_chunk, a_seq, seqs_per_group, carry_state):
    (u_ref, v_ref, q_ref, k_ref, vb_ref, r_ref, la_ref, wmix_ref, bcol_ref, gout_ref) = refs[:10]
    rest = refs[10:]
    if carry_state:
        mix_ref, sout_ref = rest[-2:]
        sin_ref = None
    else:
        sin_ref = rest[0]
        mix_ref, sout_ref = rest[-2:]

    tt = lax.broadcasted_iota(jnp.int32, (a_chunk, a_chunk), 0)
    ss = lax.broadcasted_iota(jnp.int32, (a_chunk, a_chunk), 1)
    a_mask = (ss <= tt) & ((tt // a_seq) == (ss // a_seq))
    for h in range(_HEADS):
        cols = slice(h * _A_HEAD, (h + 1) * _A_HEAD)
        wm = jnp.where(a_mask, wmix_ref[h], 0.0).astype(_BF16)
        bias = bcol_ref[h]
        for c in range(rows // a_chunk):
            rws = slice(c * a_chunk, (c + 1) * a_chunk)
            mixed = _dot(wm, v_ref[rws, cols]) + bias
            mix_ref[rws, cols] = (u_ref[rws, cols].astype(_F32) * mixed).astype(_BF16)

    gr = _GLA_ROWS
    seq_rows = gr // seqs_per_group
    ti = lax.broadcasted_iota(jnp.int32, (gr, gr), 0)
    si = lax.broadcasted_iota(jnp.int32, (gr, gr), 1)
    same_seq = (ti // seq_rows) == (si // seq_rows)
    causal = (si <= ti) & same_seq
    cum_and_total = jnp.concatenate(
        [jnp.where(causal, 1.0, 0.0), jnp.where(same_seq, 1.0, 0.0)], axis=0).astype(_BF16)
    kw = _HEADS * _DK
    row_k = lax.broadcasted_iota(jnp.int32, (gr, _DK), 0)
    row_v = lax.broadcasted_iota(jnp.int32, (gr, _DV), 0)
    pos_all = lax.broadcasted_iota(jnp.int32, (gr, kw), 0) % seq_rows
    ones_v = jnp.ones((gr, _DV), _BF16)
    scale = _DK ** -0.5
    a_width = _HEADS * _A_HEAD

    if carry_state:
        @pl.when(pl.program_id(1) == 0)
        def _():
            sout_ref[...] = jnp.zeros(sout_ref.shape, _F32)
        states = [sout_ref[h] for h in range(_HEADS)]

    for gi in range(rows // gr):
        rws = slice(gi * gr, (gi + 1) * gr)
        la = la_ref[rws, :]
        la_hi = la.astype(_BF16)
        la_lo = (la - la_hi.astype(_F32)).astype(_BF16)
        sums = _dot(cum_and_total, jnp.concatenate([la_hi, la_lo], axis=1))
        g_cum = sums[:gr, :kw] + sums[:gr, kw:]
        g_tot = sums[gr:, :kw] + sums[gr:, kw:]
        q = q_ref[rws, :].astype(_F32) * scale
        k = k_ref[rws, :].astype(_F32)
        q_in_all = (q * jnp.exp(g_cum)).astype(_BF16)
        k_in_all = (k * jnp.exp(-g_cum)).astype(_BF16)
        k_dec_all = (k * jnp.exp(g_tot - g_cum)).astype(_BF16)
        t_hi, t_mid, t_lo = _split3(g_tot)
        pieces_all = jnp.where(pos_all == 0, t_hi,
                               jnp.where(pos_all == 1, t_mid,
                                         jnp.where(pos_all == 2, t_lo, jnp.zeros_like(t_hi))))
        for h in range(_HEADS):
            kc = slice(h * _DK, (h + 1) * _DK)
            vc = slice(h * _DV, (h + 1) * _DV)
            q_in, k_in, k_dec, pieces = q_in_all[:, kc], k_in_all[:, kc], k_dec_all[:, kc], pieces_all[:, kc]
            v = vb_ref[rws, vc]
            scores = jnp.where(causal, _dot_nt(q_in, k_in), 0.0).astype(_BF16)
            if carry_state:
                o = _dot(jnp.concatenate([q_in, scores], axis=1),
                         jnp.concatenate([states[h].astype(_BF16), v], axis=0))
                decay = jnp.exp(_dot_tn(pieces, ones_v))
                states[h] = states[h] * decay + _dot_tn(k_dec, v)
            else:
                o = _dot(scores, v)
                zero = jnp.zeros_like(pieces)
                for s in range(seqs_per_group):
                    in_seq_k = (row_k // seq_rows) == s
                    in_seq_v = (row_v // seq_rows) == s
                    s0 = sin_ref[s, h]
                    o = o + jnp.where(in_seq_v, _dot(q_in, s0.astype(_BF16)), 0.0)
                    decay = jnp.exp(_dot_tn(jnp.where(in_seq_k, pieces, zero), ones_v))
                    kd = jnp.where(in_seq_k, k_dec, zero)
                    sout_ref[s, h] = s0 * decay + _dot_tn(kd, v)
            ms = jnp.mean(o * o, axis=-1, keepdims=True)
            on = (o * lax.rsqrt(ms + _EPS)) * gout_ref[:, vc]
            r = r_ref[rws, vc].astype(_F32)
            gated = on * (r * (1.0 / (1.0 + jnp.exp(-r))))
            mix_ref[rws, a_width + h * _DV:a_width + (h + 1) * _DV] = gated.astype(_BF16)

    if carry_state:
        for h in range(_HEADS):
            sout_ref[h] = states[h]


def _mixer_even(proj, loga, wmix, bcol, gout, *, layer_slot, n_slots, total_rows,
                row_start, n_seqs, seq_len, state_in=None, mix_prev=None, state_prev=None):
    carry = state_in is None
    a_width = _HEADS * _A_HEAD
    kw = _HEADS * _DK
    if carry:
        rows = 512
        blocks_per_seq = seq_len // rows
        grid = (n_seqs, blocks_per_seq)
        first = row_start // rows
        rmap = lambda col: (lambda n, c: (first + n * blocks_per_seq + c, col))
        a_chunk, a_seq, spg = _A_CHUNK, _A_CHUNK, 1
        state_shape = (n_slots, n_seqs, _HEADS, _DK, _DV)
        state_spec = pl.BlockSpec((None, None, _HEADS, _DK, _DV),
                                  lambda n, c: (layer_slot, n, 0, 0, 0))
        sem = ("arbitrary", "arbitrary")
    else:
        rows = _GLA_ROWS
        spg = rows // seq_len
        grid = (n_seqs // spg,)
        first = row_start // rows
        rmap = lambda col: (lambda g: (first + g, col))
        a_chunk, a_seq = rows, seq_len
        state_shape = (n_slots, n_seqs, _HEADS, _DK, _DV)
        state_spec = pl.BlockSpec((None, spg, _HEADS, _DK, _DV),
                                  lambda g: (layer_slot, g, 0, 0, 0))
        sem = ("arbitrary",)
    in_specs = [
        pl.BlockSpec((rows, a_width), rmap(0)),
        pl.BlockSpec((rows, a_width), rmap(1)),
        pl.BlockSpec((rows, kw), rmap(2 * a_width // kw)),
        pl.BlockSpec((rows, kw), rmap(2 * a_width // kw + 1)),
        pl.BlockSpec((rows, a_width), rmap(3)),
        pl.BlockSpec((rows, a_width), rmap(4)),
        pl.BlockSpec((rows, kw), rmap(0)),
        _layer_spec(wmix, layer_slot),
        _layer_spec(bcol, layer_slot),
        _layer_spec(gout, layer_slot),
    ]
    args = [proj, proj, proj, proj, proj, proj, loga, wmix, bcol, gout]
    if not carry:
        in_specs.append(state_spec)
        args.append(state_in)
    aliases = {}
    if mix_prev is not None:
        aliases[len(args)] = 0
        in_specs.append(pl.BlockSpec(memory_space=pl.ANY))
        args.append(mix_prev)
    if state_prev is not None:
        aliases[len(args)] = 1
        in_specs.append(pl.BlockSpec(memory_space=pl.ANY))
        args.append(state_prev)
    return pl.pallas_call(
        functools.partial(_mixer_even_kernel, rows=rows, a_chunk=a_chunk, a_seq=a_seq,
                          seqs_per_group=spg, carry_state=carry),
        grid=grid,
        in_specs=in_specs,
        out_specs=[pl.BlockSpec((rows, 2 * a_width), rmap(0)), state_spec],
        out_shape=[jax.ShapeDtypeStruct((total_rows, 2 * a_width), _BF16),
                   jax.ShapeDtypeStruct(state_shape, _F32)],
        input_output_aliases=aliases,
        compiler_params=_params(sem),
        name="mixer_even_long" if carry else "mixer_even_short",
    )(*args)


_RING = 3


def _conv_outproj_kernel(z_hbm, bg_hbm, cw_ref, hist_ref, w_ref, x_hbm, o_ref,
                         zbuf_ref, bgbuf_ref, xbuf_ref, sem_ref, zs_ref, h1_ref, h2_ref, wb_ref,
                         *, rows, halo, rows_long, seq_long, seq_short):
    i = pl.program_id(0)
    n_steps = pl.num_programs(0)
    cols = zbuf_ref.shape[2]
    keep = _CONV_TAPS - 1

    def fetch(step, slot):
        src = pl.ds(pl.multiple_of(step * rows, 8), rows)
        return (pltpu.make_async_copy(z_hbm.at[src, :], zbuf_ref.at[slot], sem_ref.at[0, slot]),
                pltpu.make_async_copy(bg_hbm.at[src, :], bgbuf_ref.at[slot], sem_ref.at[1, slot]),
                pltpu.make_async_copy(x_hbm.at[src, :], xbuf_ref.at[slot], sem_ref.at[2, slot]))

    @pl.when(i == 0)
    def _():
        for step in range(_RING - 1):
            for stream, cp in enumerate(fetch(step, step)):
                cp.start(priority=stream // 2)
        wb_ref[...] = w_ref[...].astype(_BF16)
        h1_ref[...] = jnp.zeros(h1_ref.shape, _F32)
        h2_ref[...] = jnp.zeros(h2_ref.shape, _F32)
        zs_ref[0:halo, :] = jnp.zeros((halo, cols), _F32)

    @pl.when(i + _RING - 1 < n_steps)
    def _():
        for stream, cp in enumerate(fetch(i + _RING - 1, (i + _RING - 1) % _RING)):
            cp.start(priority=stream // 2)

    slot = i % _RING
    for cp in fetch(i, slot):
        cp.wait()
    z_ref, bg_ref, x_ref = zbuf_ref.at[slot], bgbuf_ref.at[slot], xbuf_ref.at[slot]

    @pl.when((i + 1) * rows > rows_long)
    def _():
        nh = hist_ref.shape[0]
        r = i * rows + lax.broadcasted_iota(jnp.int32, (rows, nh), 0) - rows_long
        c = lax.broadcasted_iota(jnp.int32, (rows, nh), 1)
        rr = jnp.maximum(r, 0)
        n, t = rr // seq_short, rr % seq_short
        pieces = _split3(hist_ref[...])
        for back, h_ref in ((1, h1_ref), (2, h2_ref)):
            pick = (r >= 0) & (t < back) & (c == keep * n + keep - back + t)
            sel = jnp.where(pick, 1.0, 0.0).astype(_BF16)
            h_ref[...] = _dot(sel, pieces[0]) + _dot(sel, pieces[1]) + _dot(sel, pieces[2])

    zs_ref[halo:halo + rows, :] = z_ref[...].astype(_F32)
    lanes = 128
    row = i * rows + lax.broadcasted_iota(jnp.int32, (rows, lanes), 0)
    t = jnp.where(row >= rows_long, (row - rows_long) % seq_short, row % seq_long)
    has_prev1, has_prev2 = t >= 1, t >= 2

    def gated(sl):
        prev1 = jnp.where(has_prev1, zs_ref[pl.ds(halo - 1, rows), sl], h1_ref[:, sl])
        prev2 = jnp.where(has_prev2, zs_ref[pl.ds(halo - 2, rows), sl], h2_ref[:, sl])
        cur = zs_ref[pl.ds(halo, rows), sl]
        conv = prev2 * cw_ref[0:1, sl] + prev1 * cw_ref[1:2, sl] + cur * cw_ref[2:3, sl]
        return (bg_ref[:, sl].astype(_F32) * conv).astype(_BF16)

    chunk = 2 * lanes
    acc = x_ref[...]
    for kb in range(cols // chunk):
        lo = kb * chunk
        g = jnp.concatenate([gated(slice(lo, lo + lanes)), gated(slice(lo + lanes, lo + chunk))], axis=1)
        acc = acc + _dot(g, wb_ref[lo:lo + chunk, :])
    o_ref[...] = acc
    zs_ref[0:halo, :] = zs_ref[rows:rows + halo, :]


def _conv_outproj_residual(z, bg, cw, hist, w, layer, x, *, rows_long, seq_long, seq_short):
    m, c = z.shape
    n = w.shape[2]
    rows = _ROW_BLOCK // 4
    halo = 16
    assert m % rows == 0 and rows % halo == 0 and _CONV_TAPS == 3
    assert m // rows >= _RING
    in_hbm = pl.BlockSpec(memory_space=pl.ANY)
    return pl.pallas_call(
        functools.partial(_conv_outproj_kernel, rows=rows, halo=halo, rows_long=rows_long,
                          seq_long=seq_long, seq_short=seq_short),
        grid=(m // rows,),
        in_specs=[
            in_hbm,
            in_hbm,
            _layer_spec(cw, layer),
            _layer_spec(hist, layer),
            _resident_weight_spec(w, layer),
            in_hbm,
        ],
        out_specs=pl.BlockSpec((rows, n), lambda i: (i, 0)),
        out_shape=jax.ShapeDtypeStruct((m, n), _F32),
        scratch_shapes=[pltpu.VMEM((_RING, rows, c), _BF16),
                        pltpu.VMEM((_RING, rows, c), _BF16),
                        pltpu.VMEM((_RING, rows, n), _F32),
                        pltpu.SemaphoreType.DMA((3, _RING)),
                        pltpu.VMEM((rows + halo, c), _F32),
                        pltpu.VMEM((rows, c), _F32),
                        pltpu.VMEM((rows, c), _F32),
                        pltpu.VMEM((c, n), _BF16)],
        compiler_params=_params(("arbitrary",), _VMEM_LIMIT_FFN),
        name="conv_outproj_residual",
    )(z, bg, cw, hist, w, x)


def kernel(x_prompt, x_sample, state_gla, state_conv, norm_mix, norm_ffn, norm_final, w_in_even, w_gate_up, b_gate, w_spatial, b_spatial, g_gla_out, w_out_even, w_in_odd, conv_w, w_out_odd, w_ffn_up, w_ffn_down):
    n_p, t_p, d = x_prompt.shape
    n_s, t_s, _ = x_sample.shape
    rows_p, rows_s = n_p * t_p, n_s * t_s
    total = rows_p + rows_s
    depth = norm_mix.shape[0]
    n_even = w_in_even.shape[0]
    a_width = _HEADS * _A_HEAD

    x, x_tail = x_prompt.reshape(rows_p, d), x_sample.reshape(rows_s, d)

    reps = _GLA_ROWS // t_s
    wmix_s = jnp.tile(w_spatial[:, :, :t_s, :t_s], (1, 1, reps, reps))
    bcol_s = jnp.tile(b_spatial[:, :, :t_s], (1, 1, reps))[..., None]
    bcol_p = b_spatial[..., None]
    gain_mix, gain_ffn = norm_mix[:, None, :], norm_ffn[:, None, :]
    gain_final = norm_final[None, None, :]
    gain_gla = g_gla_out[:, None, :]
    bias_gate = b_gate[:, None, :]
    keep = _CONV_TAPS - 1
    hist = state_conv.astype(_F32).reshape(state_conv.shape[0], n_s * keep, d)

    w_in_even_t = jnp.swapaxes(w_in_even, 1, 2).astype(_BF16)
    w_in_odd_b = w_in_odd.astype(_BF16)
    w_gate = w_gate_up.astype(_BF16)

    v_rows, conv_p, conv_s = [], [], []
    gla_p = gla_s = None
    for l in range(depth):
        i = l // 2
        if l % 2 == 0:
            proj, v32, loga = _inproj_even(x, gain_mix, l, w_in_even_t, i, w_gate, bias_gate,
                                           tail_rows=rows_s, x_tail=x_tail)
            mix, gla_p = _mixer_even(
                proj, loga, w_spatial, bcol_p, gain_gla, layer_slot=i, n_slots=n_even,
                total_rows=total, row_start=0, n_seqs=n_p, seq_len=t_p, state_prev=gla_p)
            mix, gla_s = _mixer_even(
                proj, loga, wmix_s, bcol_s, gain_gla, layer_slot=i, n_slots=n_even,
                total_rows=total, row_start=rows_p, n_seqs=n_s, seq_len=t_s,
                state_in=state_gla, mix_prev=mix, state_prev=gla_s)
            v_rows.append(v32.reshape(n_s, t_s, a_width))
            x, x_tail = _outproj_residual(mix, w_out_even, i, x, x_tail), None
        else:
            bg, z, z_b = _inproj_odd(x, gain_mix, l, w_in_odd_b, i)
            conv_p.append(jnp.stack([z[(n + 1) * t_p - keep:(n + 1) * t_p] for n in range(n_p)]))
            conv_s.append(z[rows_p:].reshape(n_s, t_s, d)[:, t_s - keep:])
            x = _conv_outproj_residual(z_b, bg, conv_w, hist, w_out_odd, i, x,
                                       rows_long=rows_p, seq_long=t_p, seq_short=t_s)
        if l < depth - 1:
            x = _ffn_residual(x, gain_ffn, w_ffn_up, w_ffn_down, l)
        else:
            y_p, y_s = _ffn_residual(x, gain_ffn, w_ffn_up, w_ffn_down, l,
                                     final_gain=gain_final, tail_rows=rows_s)

    return (y_p.reshape(n_p, t_p, d), y_s.reshape(n_s, t_s, d), gla_p, gla_s,
            jnp.stack(conv_p), jnp.stack(conv_s), jnp.stack(v_rows))
```
